```python
import math
import jax, jax.numpy as jnp
from jax import lax
import numpy as np

D_MODEL = 1024
BATCH = 2
SEQ = 16384
DEPTH = 4

GRID_W = 64
CTX_LEN = 256
N_MIXERS = 3
EPS = 1e-6
FFN_RES = 0.5

D_FF = 2816
D_RNN = 1536
LRU_BLOCKS = 12
LRU_BW = D_RNN // LRU_BLOCKS
CONV_W = 4
LRU_C = 8.0
A_MIN = 0.9
A_MAX = 0.999

HEAD_DIM = 64
B_HEADS = 16
B_KV_HEADS = 4
WINDOW = 128
Q_BLOCK = 128
ROPE_BASE = 10000.0

C_HEADS = 16
WIN_H = 8
WIN_W = 16

NEG = -1e30

kernel_name = "hybrid_rglru_swa_nat_prefix_dit"


def rms_norm(x, g):
    xf = x.astype(jnp.float32)
    y = xf * lax.rsqrt(jnp.mean(xf * xf, axis=-1, keepdims=True) + EPS)
    return (y * g.astype(jnp.float32)).astype(x.dtype)


def adaln_in(h, g, shift, scale):
    return rms_norm(h, g) * (1.0 + scale) + shift


def swiglu(h, w_gu, w_dn):
    g, u = jnp.split(h @ w_gu, 2, axis=-1)
    return (jax.nn.silu(g) * u) @ w_dn


def ffn_residual(h, g, shift, scale, gate, w_gu, w_dn):
    return h + FFN_RES * gate * swiglu(adaln_in(h, g, shift, scale), w_gu, w_dn)


def joint_softmax(logits):
    p = jax.nn.softmax(jnp.concatenate(logits, axis=-1).astype(jnp.float32), axis=-1)
    cuts = [int(n) for n in np.cumsum([l.shape[-1] for l in logits[:-1]])]
    return jnp.split(p, cuts, axis=-1)


def axial_rope(t):
    L = t.shape[1]
    pos = jnp.arange(L)
    half = HEAD_DIM // 2
    quarter = half // 2
    inv = ROPE_BASE ** (-jnp.arange(quarter, dtype=jnp.float32) / quarter)
    tf = t.astype(jnp.float32)
    bshape = (L,) + (1,) * (t.ndim - 3) + (quarter,)
    outs = []
    for axis_pos, part in ((pos // GRID_W, tf[..., :half]), (pos % GRID_W, tf[..., half:])):
        ang = axis_pos.astype(jnp.float32)[:, None] * inv
        cos = jnp.cos(ang).reshape(bshape)
        sin = jnp.sin(ang).reshape(bshape)
        x1, x2 = part[..., :quarter], part[..., quarter:]
        outs += [x1 * cos - x2 * sin, x2 * cos + x1 * sin]
    return jnp.concatenate(outs, axis=-1).astype(t.dtype)


def dw_conv_centred(x, w, b):
    L = x.shape[1]
    left = (CONV_W - 1) // 2
    xp = jnp.pad(x, ((0, 0), (left, CONV_W - 1 - left), (0, 0)))
    y = b
    for k in range(CONV_W):
        y = y + xp[:, k:k + L] * w[k]
    return y


def block_diag(x, w, b):
    xb = x.reshape(x.shape[:-1] + (LRU_BLOCKS, LRU_BW))
    return jnp.einsum('blnk,nkj->blnj', xb, w).reshape(x.shape) + b


def _lin_combine(e1, e2):
    a1, b1 = e1
    a2, b2 = e2
    return a1 * a2, a2 * b1 + b2


def rglru_scan(xc, w_g, b_g, lam, h0, reverse):
    xf = xc.astype(jnp.float32)
    r = jax.nn.sigmoid(block_diag(xf, w_g[0], b_g[0]))
    i = jax.nn.sigmoid(block_diag(xf, w_g[1], b_g[1]))
    log_a = -LRU_C * r * jax.nn.softplus(-lam.astype(jnp.float32))
    a = jnp.exp(log_a)
    b = jnp.sqrt(-jnp.expm1(2.0 * log_a)) * (i * xf)
    edge = -1 if reverse else 0
    b = b.at[:, edge].add(a[:, edge] * h0)
    _, h = lax.associative_scan(_lin_combine, (a, b), axis=1, reverse=reverse)
    return h


def mixer_rglru(hx, hc, w_in, conv_w, conv_b, gate_w, gate_b, lam, w_out, need_ctx):
    def split_in(h):
        g, xr = jnp.split(h @ w_in, 2, axis=-1)
        return g, dw_conv_centred(xr, conv_w, conv_b)
    g_x, xc_x = split_in(hx)
    g_c, xc_c = split_in(hc)
    h0 = jnp.zeros((hx.shape[0], D_RNN), jnp.float32)
    hf_c = rglru_scan(xc_c, gate_w[0], gate_b[0], lam[0], h0, reverse=False)
    hb_c = rglru_scan(xc_c, gate_w[1], gate_b[1], lam[1], h0, reverse=True)
    hf_x = rglru_scan(xc_x, gate_w[0], gate_b[0], lam[0], hf_c[:, -1], reverse=False)
    hb_x = rglru_scan(xc_x, gate_w[1], gate_b[1], lam[1], hb_c[:, 0], reverse=True)
    y_x = ((hf_x + hb_x).astype(hx.dtype) * jax.nn.gelu(g_x)) @ w_out
    y_c = None
    if need_ctx:
        y_c = ((hf_c + hb_c).astype(hc.dtype) * jax.nn.gelu(g_c)) @ w_out
    return y_x, y_c


def mixer_swa(hx, hc, w_qkv, sinks, w_o, need_ctx):
    bsz, S, _ = hx.shape
    G, R = B_KV_HEADS, B_HEADS // B_KV_HEADS
    scale = HEAD_DIM ** -0.5

    def proj(h):
        L = h.shape[1]
        q, k, v = jnp.split(h @ w_qkv, [B_HEADS * HEAD_DIM, (B_HEADS + B_KV_HEADS) * HEAD_DIM], axis=-1)
        return (q.reshape(bsz, L, G, R, HEAD_DIM), k.reshape(bsz, L, G, HEAD_DIM),
                v.reshape(bsz, L, G, HEAD_DIM))

    qx, kx, vx = proj(hx)
    qc, kc, vc = proj(hc)
    qx = axial_rope(qx) * scale
    kx = axial_rope(kx)
    qc = qc * scale
    sink = sinks.astype(jnp.float32).reshape(G, R, 1, 1)

    nb = S // Q_BLOCK
    span = Q_BLOCK + 2 * WINDOW
    kp = jnp.pad(kx, ((0, 0), (WINDOW, WINDOW), (0, 0), (0, 0)))
    vp = jnp.pad(vx, ((0, 0), (WINDOW, WINDOW), (0, 0), (0, 0)))
    qb = jnp.moveaxis(qx.reshape(bsz, nb, Q_BLOCK, G, R, HEAD_DIM), 1, 0)

    def block(args):
        jb, q = args
        start = jb * Q_BLOCK
        k = lax.dynamic_slice_in_dim(kp, start, span, axis=1)
        v = lax.dynamic_slice_in_dim(vp, start, span, axis=1)
        q_pos = start + jnp.arange(Q_BLOCK)
        k_pos = start - WINDOW + jnp.arange(span)
        valid = ((jnp.abs(q_pos[:, None] - k_pos[None, :]) <= WINDOW)
                 & (k_pos >= 0)[None, :] & (k_pos < S)[None, :])
        s_loc = jnp.where(valid, jnp.einsum('bqgrd,bkgd->bgrqk', q, k).astype(jnp.float32), NEG)
        s_ctx = jnp.einsum('bqgrd,bkgd->bgrqk', q, kc).astype(jnp.float32)
        s_sink = jnp.broadcast_to(sink, s_ctx.shape[:-1] + (1,))
        _, p_ctx, p_loc = joint_softmax([s_sink, s_ctx, s_loc])
        return (jnp.einsum('bgrqk,bkgd->bqgrd', p_loc.astype(v.dtype), v)
                + jnp.einsum('bgrqk,bkgd->bqgrd', p_ctx.astype(vc.dtype), vc))

    o = lax.map(block, (jnp.arange(nb), qb))
    y_x = jnp.moveaxis(o, 0, 1).reshape(bsz, S, B_HEADS * HEAD_DIM) @ w_o
    y_c = None
    if need_ctx:
        s_cc = jnp.einsum('bqgrd,bkgd->bgrqk', qc, kc).astype(jnp.float32)
        _, p_cc = joint_softmax([jnp.broadcast_to(sink, s_cc.shape[:-1] + (1,)), s_cc])
        o_c = jnp.einsum('bgrqk,bkgd->bqgrd', p_cc.astype(vc.dtype), vc)
        y_c = o_c.reshape(bsz, hc.shape[1], B_HEADS * HEAD_DIM) @ w_o
    return y_x, y_c


def mixer_nat(hx, hc, w_qkv, rpb, w_o, need_ctx):
    bsz, S, _ = hx.shape
    H = C_HEADS
    rows = S // GRID_W
    kh = min(WIN_H, rows)
    kw = WIN_W
    scale = HEAD_DIM ** -0.5

    def proj(h):
        L = h.shape[1]
        q, k, v = jnp.split(h @ w_qkv, 3, axis=-1)
        return (q.reshape(bsz, L, H, HEAD_DIM) * scale, k.reshape(bsz, L, H, HEAD_DIM),
                v.reshape(bsz, L, H, HEAD_DIM))

    qx, kx, vx = proj(hx)
    qc, kc, vc = proj(hc)
    qg = jnp.moveaxis(qx.reshape(bsz, rows, GRID_W, H, HEAD_DIM), 1, 0)
    kg = kx.reshape(bsz, rows, GRID_W, H, HEAD_DIM)
    vg = vx.reshape(bsz, rows, GRID_W, H, HEAD_DIM)
    col = np.arange(GRID_W)
    col_start = np.clip(col - kw // 2, 0, GRID_W - kw)
    col_idx = col_start[:, None] + np.arange(kw)
    col_off = col_idx - col[:, None] + (WIN_W - 1)
    rpb_cols = rpb[:, :, col_off]

    def row_block(args):
        r, q = args
        rs = jnp.clip(r - kh // 2, 0, rows - kh)
        k_rows = lax.dynamic_slice_in_dim(kg, rs, kh, axis=1)
        v_rows = lax.dynamic_slice_in_dim(vg, rs, kh, axis=1)
        k_nb = jnp.moveaxis(k_rows[:, :, col_idx], 2, 1).reshape(bsz, GRID_W, kh * kw, H, HEAD_DIM)
        v_nb = jnp.moveaxis(v_rows[:, :, col_idx], 2, 1).reshape(bsz, GRID_W, kh * kw, H, HEAD_DIM)
        row_off = rs + jnp.arange(kh) - r + (WIN_H - 1)
        bias = jnp.moveaxis(jnp.take(rpb_cols, row_off, axis=1), 1, 2).reshape(H, GRID_W, kh * kw)
        s_nb = jnp.einsum('bqhd,bqkhd->bhqk', q, k_nb).astype(jnp.float32) + bias.astype(jnp.float32)
        s_ctx = jnp.einsum('bqhd,bkhd->bhqk', q, kc).astype(jnp.float32)
        p_ctx, p_nb = joint_softmax([s_ctx, s_nb])
        return (jnp.einsum('bhqk,bqkhd->bqhd', p_nb.astype(v_nb.dtype), v_nb)
                + jnp.einsum('bhqk,bkhd->bqhd', p_ctx.astype(vc.dtype), vc))

    o = lax.map(row_block, (jnp.arange(rows), qg))
    y_x = jnp.moveaxis(o, 0, 1).reshape(bsz, S, H * HEAD_DIM) @ w_o
    y_c = None
    if need_ctx:
        s_cc = jnp.einsum('bqhd,bkhd->bhqk', qc, kc).astype(jnp.float32)
        p_cc = jax.nn.softmax(s_cc, axis=-1)
        o_c = jnp.einsum('bhqk,bkhd->bqhd', p_cc.astype(vc.dtype), vc)
        y_c = o_c.reshape(bsz, hc.shape[1], H * HEAD_DIM) @ w_o
    return y_x, y_c


def setup_inputs(seed: int = 0) -> dict:
    key = jax.random.key(seed)
    ks = jax.random.split(key, 24)
    D = D_MODEL
    n_a = len(range(0, DEPTH, N_MIXERS))
    n_b = len(range(1, DEPTH, N_MIXERS))
    n_c = len(range(2, DEPTH, N_MIXERS))

    def nrm(k, shape, s):
        return jax.random.normal(k, shape, jnp.float32) * s

    u = jax.random.uniform(ks[13], (n_a, 2, D_RNN), jnp.float32, A_MIN, A_MAX)
    a_base = u ** (1.0 / LRU_C)
    lam = jnp.log(a_base) - jnp.log1p(-a_base)
    return {
        "x": nrm(ks[0], (BATCH, SEQ, D), 1.0),
        "c": nrm(ks[1], (BATCH, D), 1.0),
        "ctx": nrm(ks[2], (BATCH, CTX_LEN, D), 1.0),
        "c_ctx": nrm(ks[3], (D,), 1.0),
        "w_ada": nrm(ks[4], (DEPTH, D, 9 * D), 0.5 * D ** -0.5),
        "b_ada": nrm(ks[5], (DEPTH, 9 * D), 0.02),
        "norm_g": 1.0 + nrm(ks[6], (DEPTH, 3, D), 0.1),
        "w_ffn_gu": nrm(ks[7], (DEPTH, 2, D, 2 * D_FF), D ** -0.5),
        "w_ffn_down": nrm(ks[8], (DEPTH, 2, D_FF, D), D_FF ** -0.5),
        "a_w_in": nrm(ks[9], (n_a, D, 2 * D_RNN), D ** -0.5),
        "a_conv_w": nrm(ks[10], (n_a, CONV_W, D_RNN), CONV_W ** -0.5),
        "a_conv_b": nrm(ks[11], (n_a, D_RNN), 0.02),
        "a_gate_w": nrm(ks[12], (n_a, 2, 2, LRU_BLOCKS, LRU_BW, LRU_BW), LRU_BW ** -0.5),
        "a_gate_b": nrm(ks[14], (n_a, 2, 2, D_RNN), 0.1),
        "a_lambda": lam,
        "a_w_out": nrm(ks[15], (n_a, D_RNN, D), D_RNN ** -0.5),
        "b_w_qkv": nrm(ks[16], (n_b, D, (B_HEADS + 2 * B_KV_HEADS) * HEAD_DIM), D ** -0.5),
        "b_sinks": nrm(ks[17], (n_b, B_HEADS), 1.0),
        "b_w_o": nrm(ks[18], (n_b, B_HEADS * HEAD_DIM, D), (B_HEADS * HEAD_DIM) ** -0.5),
        "c_w_qkv": nrm(ks[19], (n_c, D, 3 * C_HEADS * HEAD_DIM), D ** -0.5),
        "c_rpb": nrm(ks[20], (n_c, C_HEADS, 2 * WIN_H - 1, 2 * WIN_W - 1), 0.5),
        "c_w_o": nrm(ks[21], (n_c, C_HEADS * HEAD_DIM, D), (C_HEADS * HEAD_DIM) ** -0.5),
        "final_g": 1.0 + nrm(ks[22], (D,), 0.1),
    }


def reference(x, c, ctx, c_ctx, w_ada, b_ada, norm_g, w_ffn_gu, w_ffn_down,
              a_w_in, a_conv_w, a_conv_b, a_gate_w, a_gate_b, a_lambda, a_w_out,
              b_w_qkv, b_sinks, b_w_o, c_w_qkv, c_rpb, c_w_o, final_g):
    bsz = x.shape[0]
    sc = jax.nn.silu(c)
    scc = jax.nn.silu(c_ctx)
    for i in range(DEPTH):
        kind, j = i % N_MIXERS, i // N_MIXERS
        need_ctx = i < DEPTH - 1
        mx = (sc @ w_ada[i] + b_ada[i]).reshape(bsz, 3, 3, 1, D_MODEL)
        mc = (scc @ w_ada[i] + b_ada[i]).reshape(3, 3, D_MODEL)
        x = ffn_residual(x, norm_g[i, 0], mx[:, 0, 0], mx[:, 0, 1], mx[:, 0, 2], w_ffn_gu[i, 0], w_ffn_down[i, 0])
        ctx = ffn_residual(ctx, norm_g[i, 0], mc[0, 0], mc[0, 1], mc[0, 2], w_ffn_gu[i, 0], w_ffn_down[i, 0])
        hx = adaln_in(x, norm_g[i, 1], mx[:, 1, 0], mx[:, 1, 1])
        hc = adaln_in(ctx, norm_g[i, 1], mc[1, 0], mc[1, 1])
        if kind == 0:
            y_x, y_c = mixer_rglru(hx, hc, a_w_in[j], a_conv_w[j], a_conv_b[j], a_gate_w[j],
                                   a_gate_b[j], a_lambda[j], a_w_out[j], need_ctx)
        elif kind == 1:
            y_x, y_c = mixer_swa(hx, hc, b_w_qkv[j], b_sinks[j], b_w_o[j], need_ctx)
        else:
            y_x, y_c = mixer_nat(hx, hc, c_w_qkv[j], c_rpb[j], c_w_o[j], need_ctx)
        x = x + mx[:, 1, 2] * y_x
        x = ffn_residual(x, norm_g[i, 2], mx[:, 2, 0], mx[:, 2, 1], mx[:, 2, 2], w_ffn_gu[i, 1], w_ffn_down[i, 1])
        if need_ctx:
            ctx = ctx + mc[1, 2] * y_c
            ctx = ffn_residual(ctx, norm_g[i, 2], mc[2, 0], mc[2, 1], mc[2, 2], w_ffn_gu[i, 1], w_ffn_down[i, 1])
    return rms_norm(x, final_g)
```

```python
import functools

import numpy as np
import jax
import jax.numpy as jnp
from jax import lax
from jax.experimental import pallas as pl
from jax.experimental.pallas import tpu as pltpu

D_MODEL = 1024
BATCH = 2
SEQ = 16384
DEPTH = 4
GRID_W = 64
CTX_LEN = 256
N_MIXERS = 3
EPS = 1e-6
FFN_RES = 0.5
D_FF = 2816
D_RNN = 1536
LRU_BLOCKS = 12
LRU_BW = D_RNN // LRU_BLOCKS
CONV_W = 4
LRU_C = 8.0
HEAD_DIM = 64
B_HEADS = 16
B_KV_HEADS = 4
WINDOW = 128
ROPE_BASE = 10000.0
C_HEADS = 16
WIN_H = 8
WIN_W = 16
NEG = -1e30

LANES = 128
SUBLANES = 8
VMEM_LIMIT = 56 * 1024 * 1024

N_X = BATCH * SEQ
N_CTX = BATCH * CTX_LEN
N_ALL = N_X + N_CTX
TM = 512
N_TILES = N_ALL // TM
X_TILES = N_X // TM
TILES_PER_BATCH = SEQ // TM
MOD_ROWS = 16
FF_CHUNK = 256
N_FF_CHUNKS = D_FF // FF_CHUNK

F32 = jnp.float32
BF16 = jnp.bfloat16


def _cparams(sem):
    return pltpu.CompilerParams(dimension_semantics=sem, vmem_limit_bytes=VMEM_LIMIT)


def _const_spec(shape):
    nd = len(shape)
    return pl.BlockSpec(shape, lambda *_: (0,) * nd, pipeline_mode=pl.Buffered(1))


def _tile_kind(i):
    return jnp.minimum(i // TILES_PER_BATCH, BATCH)


def _adaln(x, mod_ref, ng_ref, sub):
    g = ng_ref[sub:sub + 1, :]
    shift = mod_ref[3 * sub:3 * sub + 1, :]
    scale = mod_ref[3 * sub + 1:3 * sub + 2, :]
    y = x * lax.rsqrt(jnp.mean(x * x, axis=-1, keepdims=True) + EPS)
    return (y * g) * (1.0 + scale) + shift


ADA_COLS = 1152


def _ada_kernel(c_ref, w_ref, b_ref, o_ref):
    c = c_ref[...]
    sc = c * jax.nn.sigmoid(c)
    o_ref[...] = jnp.dot(sc, w_ref[...], preferred_element_type=F32,
                         precision=lax.Precision.HIGHEST) + b_ref[...]


def _ada_mods(cvec, w_ada, b_ada):
    out = pl.pallas_call(
        _ada_kernel,
        grid=(DEPTH, 9 * D_MODEL // ADA_COLS),
        in_specs=[
            pl.BlockSpec((SUBLANES, D_MODEL), lambda i, j: (0, 0)),
            pl.BlockSpec((None, D_MODEL, ADA_COLS), lambda i, j: (i, 0, j)),
            pl.BlockSpec((None, 1, ADA_COLS), lambda i, j: (i, 0, j)),
        ],
        out_specs=pl.BlockSpec((None, SUBLANES, ADA_COLS), lambda i, j: (i, 0, j)),
        out_shape=jax.ShapeDtypeStruct((DEPTH, SUBLANES, 9 * D_MODEL), F32),
        compiler_params=_cparams(("arbitrary", "arbitrary")),
        name="ada_mods",
    )(cvec, w_ada, b_ada.reshape(DEPTH, 1, 9 * D_MODEL))
    mods = out.reshape(DEPTH, SUBLANES, 9, D_MODEL)[:, :BATCH + 1]
    return jnp.pad(mods, ((0, 0), (0, 0), (0, MOD_ROWS - 9), (0, 0)))


def _ffn_kernel(*refs, sub, final):
    if final:
        x_ref, mod_ref, ng_ref, wgu_ref, wd_ref, fg_ref, o_ref, hb_ref, acc_ref = refs
    else:
        x_ref, mod_ref, ng_ref, wgu_ref, wd_ref, o_ref, hb_ref, acc_ref = refs
    x = x_ref[...]
    hb_ref[...] = _adaln(x, mod_ref, ng_ref, sub).astype(BF16)
    acc_ref[...] = jnp.zeros_like(acc_ref)

    def body(c, carry):
        gu = jnp.dot(hb_ref[...], wgu_ref[c], preferred_element_type=F32)
        g = gu[:, :FF_CHUNK]
        u = gu[:, FF_CHUNK:]
        a = ((g * jax.nn.sigmoid(g)) * u).astype(BF16)
        acc_ref[...] += jnp.dot(a, wd_ref[c], preferred_element_type=F32)
        return carry

    lax.fori_loop(0, N_FF_CHUNKS, body, 0)
    gate = mod_ref[3 * sub + 2:3 * sub + 3, :]
    y = x + (FFN_RES * gate) * acc_ref[...]
    if final:
        y = (y * lax.rsqrt(jnp.mean(y * y, axis=-1, keepdims=True) + EPS)) * fg_ref[...]
    o_ref[...] = y


def _ffn(xs, mods_i, ng_i, wgu, wd, sub, final_g=None):
    final = final_g is not None
    n_tiles = X_TILES if final else N_TILES
    in_specs = [
        pl.BlockSpec((TM, D_MODEL), lambda i: (i, 0)),
        pl.BlockSpec((None, MOD_ROWS, D_MODEL), lambda i: (_tile_kind(i), 0, 0)),
        _const_spec((SUBLANES, D_MODEL)),
        _const_spec((N_FF_CHUNKS, D_MODEL, 2 * FF_CHUNK)),
        _const_spec((N_FF_CHUNKS, FF_CHUNK, D_MODEL)),
    ]
    args = [xs, mods_i, ng_i, wgu, wd]
    if final:
        in_specs.append(_const_spec((1, D_MODEL)))
        args.append(final_g.reshape(1, D_MODEL))
    return pl.pallas_call(
        functools.partial(_ffn_kernel, sub=sub, final=final),
        grid=(n_tiles,),
        in_specs=in_specs,
        out_specs=pl.BlockSpec((TM, D_MODEL), lambda i: (i, 0)),
        out_shape=jax.ShapeDtypeStruct((n_tiles * TM, D_MODEL), F32),
        scratch_shapes=[pltpu.VMEM((TM, D_MODEL), BF16), pltpu.VMEM((TM, D_MODEL), F32)],
        compiler_params=_cparams(("arbitrary",)),
        name="ffn_final" if final else "ffn",
    )(*args)


def _prep_ffn_weights(w_gu, w_dn):
    wg = w_gu[:, :D_FF].reshape(D_MODEL, N_FF_CHUNKS, FF_CHUNK)
    wu = w_gu[:, D_FF:].reshape(D_MODEL, N_FF_CHUNKS, FF_CHUNK)
    wgu = jnp.concatenate([wg, wu], axis=-1).transpose(1, 0, 2).astype(BF16)
    wd = w_dn.reshape(N_FF_CHUNKS, FF_CHUNK, D_MODEL).astype(BF16)
    return wgu, wd


NMM_COLS = 512


def _rope_tile(y, c, s1, s2):
    return y * c + pltpu.roll(y, LANES - 16, 1) * s1 + pltpu.roll(y, 16, 1) * s2


def _nmm_kernel(*refs, n_cols, rope_cols, out_dtype):
    if rope_cols:
        x_ref, mod_ref, ng_ref, w_ref, c_ref, s1_ref, s2_ref, o_ref = refs
    else:
        x_ref, mod_ref, ng_ref, w_ref, o_ref = refs
    hb = _adaln(x_ref[...], mod_ref, ng_ref, 1).astype(BF16)
    for c0 in range(0, n_cols, NMM_COLS):
        y = jnp.dot(hb, w_ref[:, c0:c0 + NMM_COLS], preferred_element_type=F32)
        if c0 < rope_cols:
            c, s1, s2 = c_ref[...], s1_ref[...], s2_ref[...]
            y = jnp.concatenate(
                [_rope_tile(y[:, t:t + LANES], c, s1, s2) for t in range(0, NMM_COLS, LANES)], axis=1)
        o_ref[:, c0:c0 + NMM_COLS] = y.astype(out_dtype)


def _nmm(xs, mods_i, ng_i, w, out_dtype, rope=None, rope_cols=0):
    n_cols = w.shape[1]
    in_specs = [
        pl.BlockSpec((TM, D_MODEL), lambda i: (i, 0)),
        pl.BlockSpec((None, MOD_ROWS, D_MODEL), lambda i: (_tile_kind(i), 0, 0)),
        _const_spec((SUBLANES, D_MODEL)),
        _const_spec((D_MODEL, n_cols)),
    ]
    args = [xs, mods_i, ng_i, w]
    if rope_cols:
        rmap = lambda i: (jnp.where(i < X_TILES, i % TILES_PER_BATCH, TILES_PER_BATCH), 0)
        in_specs += [pl.BlockSpec((TM, LANES), rmap)] * 3
        args += list(rope)
    return pl.pallas_call(
        functools.partial(_nmm_kernel, n_cols=n_cols, rope_cols=rope_cols, out_dtype=out_dtype),
        grid=(N_TILES,),
        in_specs=in_specs,
        out_specs=pl.BlockSpec((TM, n_cols), lambda i: (i, 0)),
        out_shape=jax.ShapeDtypeStruct((N_ALL, n_cols), out_dtype),
        compiler_params=_cparams(("arbitrary",)),
        name="adaln_proj",
    )(*args)


def _rope_tables():
    quarter = HEAD_DIM // 4
    pos = jnp.arange(SEQ)
    inv = ROPE_BASE ** (-jnp.arange(quarter, dtype=F32) / quarter)
    lane = np.arange(LANES)
    d = lane % HEAD_DIM
    use_col = (d >= HEAD_DIM // 2)
    first = (d % (HEAD_DIM // 2)) < quarter
    ang_row = (pos // GRID_W).astype(F32)[:, None] * inv
    ang_col = (pos % GRID_W).astype(F32)[:, None] * inv
    fi = d % quarter
    ang = jnp.where(use_col[None, :], ang_col[:, fi], ang_row[:, fi])
    cos, sin = jnp.cos(ang), jnp.sin(ang)
    s1 = jnp.where(first[None, :], -sin, 0.0)
    s2 = jnp.where(first[None, :], 0.0, sin)
    ident = jnp.ones((TM, LANES), F32)
    zero = jnp.zeros((TM, LANES), F32)
    return (jnp.concatenate([cos, ident]), jnp.concatenate([s1, zero]), jnp.concatenate([s2, zero]))


def _proj_res_attn_kernel(x_ref, a_ref, mod_ref, w_ref, o_ref):
    y = jnp.dot(a_ref[...], w_ref[...], preferred_element_type=F32)
    o_ref[...] = x_ref[...] + mod_ref[5:6, :] * y


def _gelu_tanh(g):
    return 0.5 * g * (1.0 + jnp.tanh(np.sqrt(2.0 / np.pi).astype(np.float32) * (g + 0.044715 * (g * g * g))))


def _proj_res_lru_kernel(x_ref, hf_ref, hb_ref, g_ref, mod_ref, w_ref, o_ref):
    a = ((hf_ref[...] + hb_ref[...]) * _gelu_tanh(g_ref[...])).astype(BF16)
    y = jnp.dot(a, w_ref[...], preferred_element_type=F32)
    o_ref[...] = x_ref[...] + mod_ref[5:6, :] * y


def _proj_res(xs, mods_i, w, attn=None, lru=None):
    x_spec = pl.BlockSpec((TM, D_MODEL), lambda i: (i, 0))
    mod_spec = pl.BlockSpec((None, MOD_ROWS, D_MODEL), lambda i: (_tile_kind(i), 0, 0))
    if attn is not None:
        kern = _proj_res_attn_kernel
        in_specs = [x_spec, pl.BlockSpec((TM, D_MODEL), lambda i: (i, 0)), mod_spec,
                    _const_spec((D_MODEL, D_MODEL))]
        args = [xs, attn, mods_i, w]
    else:
        hf, hb, u = lru
        kern = _proj_res_lru_kernel
        h_spec = pl.BlockSpec((TM, D_RNN), lambda i: (i, 0))
        in_specs = [x_spec, h_spec, h_spec, h_spec, mod_spec, _const_spec((D_RNN, D_MODEL))]
        args = [xs, hf, hb, u, mods_i, w]
    return pl.pallas_call(
        kern,
        grid=(N_TILES,),
        in_specs=in_specs,
        out_specs=pl.BlockSpec((TM, D_MODEL), lambda i: (i, 0)),
        out_shape=jax.ShapeDtypeStruct((N_ALL, D_MODEL), F32),
        compiler_params=_cparams(("arbitrary",)),
        name="proj_res",
    )(*args)


LRU_T = 256
LRU_SEG = LRU_T // SUBLANES
X_CHUNKS = SEQ // LRU_T
HALO = SUBLANES


def _softplus(x):
    return jnp.maximum(x, 0.0) + jnp.log1p(jnp.exp(-jnp.abs(x)))


def _lru_kernel(cur_ref, prev_ref, next_ref, cw_ref, gw_ref, gb_ref, o_ref,
                ext_ref, a_ref, b_ref, p_ref, cm_ref, carry_ref, *, reverse):
    m = pl.program_id(1)
    is_x = m > 0
    c = jnp.where(reverse, X_CHUNKS - m, m - 1)

    @pl.when(m == 0)
    def _():
        carry_ref[...] = jnp.zeros_like(carry_ref)

    has_prev = jnp.logical_and(is_x, c > 0)
    has_next = jnp.logical_and(is_x, c < X_CHUNKS - 1)
    ext_ref[0:HALO, :] = jnp.where(has_prev, prev_ref[...], 0.0)
    ext_ref[HALO:HALO + LRU_T, :] = cur_ref[...]
    ext_ref[HALO + LRU_T:, :] = jnp.where(has_next, next_ref[...], 0.0)

    for n in range(LRU_BLOCKS):
        cols = slice(n * LRU_BW, (n + 1) * LRU_BW)
        xc = cw_ref[CONV_W:CONV_W + 1, cols]
        for k in range(CONV_W):
            xc = xc + ext_ref[HALO - 1 + k:HALO - 1 + k + LRU_T, cols] * cw_ref[k:k + 1, cols]
        xcb = xc.astype(BF16)
        r = jax.nn.sigmoid(jnp.dot(xcb, gw_ref[0, n], preferred_element_type=F32) + gb_ref[0:1, cols])
        i = jax.nn.sigmoid(jnp.dot(xcb, gw_ref[1, n], preferred_element_type=F32) + gb_ref[1:2, cols])
        log_a = (-LRU_C * r) * _softplus(-gb_ref[2:3, cols])
        a_ref[n] = jnp.exp(log_a)
        t = jnp.tanh(log_a)
        b_ref[n] = jnp.sqrt((-2.0 * t) / (1.0 - t)) * (i * xc)

    hs = [jnp.zeros((SUBLANES, LRU_BW), F32)] * LRU_BLOCKS
    ps = [jnp.ones((SUBLANES, LRU_BW), F32)] * LRU_BLOCKS
    steps = range(LRU_SEG - 1, -1, -1) if reverse else range(LRU_SEG)
    for k in steps:
        rows = pl.ds(k, SUBLANES, stride=LRU_SEG)
        for n in range(LRU_BLOCKS):
            a = a_ref[n, rows, :]
            hs[n] = a * hs[n] + b_ref[n, rows, :]
            ps[n] = a * ps[n]
            b_ref[n, rows, :] = hs[n]
            p_ref[n, rows, :] = ps[n]

    segs = range(SUBLANES - 1, -1, -1) if reverse else range(SUBLANES)
    for n in range(LRU_BLOCKS):
        st = carry_ref[n, 0:1, :]
        for s in segs:
            cm_ref[n, s:s + 1, :] = st
            st = ps[n][s:s + 1, :] * st + hs[n][s:s + 1, :]
        carry_ref[n, 0:1, :] = st

    for n in range(LRU_BLOCKS):
        cols = slice(n * LRU_BW, (n + 1) * LRU_BW)
        for s in range(SUBLANES):
            rows = slice(s * LRU_SEG, (s + 1) * LRU_SEG)
            o_ref[rows, cols] = b_ref[n, rows, :] + p_ref[n, rows, :] * cm_ref[n, s:s + 1, :]


def _lru_scan(u, cw, gw, gb, reverse):
    blocks_per_chunk = LRU_T // HALO
    n_halo_blocks = N_ALL // HALO

    def row_block(b, m):
        c = (X_CHUNKS - m) if reverse else (m - 1)
        return jnp.where(m == 0, N_X // LRU_T + b, b * X_CHUNKS + c)

    cur = lambda b, m: (row_block(b, m), 1)
    prev = lambda b, m: (jnp.maximum(row_block(b, m) * blocks_per_chunk - 1, 0), 1)
    nxt = lambda b, m: (jnp.minimum((row_block(b, m) + 1) * blocks_per_chunk, n_halo_blocks - 1), 1)
    return pl.pallas_call(
        functools.partial(_lru_kernel, reverse=reverse),
        grid=(BATCH, X_CHUNKS + 1),
        in_specs=[
            pl.BlockSpec((LRU_T, D_RNN), cur),
            pl.BlockSpec((HALO, D_RNN), prev),
            pl.BlockSpec((HALO, D_RNN), nxt),
            _const_spec((SUBLANES, D_RNN)),
            _const_spec((2, LRU_BLOCKS, LRU_BW, LRU_BW)),
            _const_spec((SUBLANES, D_RNN)),
        ],
        out_specs=pl.BlockSpec((LRU_T, D_RNN), lambda b, m: (row_block(b, m), 0)),
        out_shape=jax.ShapeDtypeStruct((N_ALL, D_RNN), F32),
        scratch_shapes=[
            pltpu.VMEM((LRU_T + 2 * HALO, D_RNN), F32),
            pltpu.VMEM((LRU_BLOCKS, LRU_T, LRU_BW), F32),
            pltpu.VMEM((LRU_BLOCKS, LRU_T, LRU_BW), F32),
            pltpu.VMEM((LRU_BLOCKS, LRU_T, LRU_BW), F32),
            pltpu.VMEM((LRU_BLOCKS, SUBLANES, LRU_BW), F32),
            pltpu.VMEM((LRU_BLOCKS, SUBLANES, LRU_BW), F32),
        ],
        compiler_params=_cparams(("arbitrary", "arbitrary")),
        name="lru_bwd" if reverse else "lru_fwd",
    )(u, u, u, cw, gw, gb)


SWA_TQ = 128
SWA_XQ = SEQ // SWA_TQ
SWA_CQ = CTX_LEN // SWA_TQ
SWA_SPAN = 3 * SWA_TQ
KV_COLS = 2 * B_KV_HEADS * LANES
_NT = (((1,), (1,)), ((), ()))


def _split_heads(qt):
    lo = lax.broadcasted_iota(jnp.int32, qt.shape, 1) < HEAD_DIM
    zero = jnp.zeros_like(qt)
    return jnp.concatenate([jnp.where(lo, qt, zero), jnp.where(lo, zero, qt)], axis=0)


def _merge_heads(o):
    t = o.shape[0] // 2
    lo = lax.broadcasted_iota(jnp.int32, (t, LANES), 1) < HEAD_DIM
    return jnp.where(lo, o[:t], o[t:])


def _swa_kernel(sink_ref, q_ref, kvp_ref, kvc_ref, kvn_ref, kvx_ref, o_ref):
    j = pl.program_id(1)
    iq = lax.broadcasted_iota(jnp.int32, (SWA_TQ, SWA_SPAN), 0)
    ik = lax.broadcasted_iota(jnp.int32, (SWA_TQ, SWA_SPAN), 1)
    lo_ok = jnp.where(j > 0, 0, SWA_TQ)
    hi_ok = jnp.where(j < SWA_XQ - 1, SWA_SPAN, 2 * SWA_TQ)
    hi_ok = jnp.where(j < SWA_XQ, hi_ok, 0)
    dist = iq + WINDOW - ik
    bias1 = jnp.where(jnp.abs(dist) <= WINDOW, 0.0, NEG)
    bias1 = jnp.where(ik >= lo_ok, bias1, NEG)
    bias1 = jnp.where(ik < hi_ok, bias1, NEG)
    bias = jnp.concatenate([bias1, bias1], axis=0)
    top = lax.broadcasted_iota(jnp.int32, (2 * SWA_TQ, 1), 0) < SWA_TQ
    for g in range(B_KV_HEADS):
        kcol = slice(g * LANES, (g + 1) * LANES)
        vcol = slice((B_KV_HEADS + g) * LANES, (B_KV_HEADS + g + 1) * LANES)
        kd = jnp.concatenate([kvp_ref[:, kcol], kvc_ref[:, kcol], kvn_ref[:, kcol]], axis=0)
        vd = jnp.concatenate([kvp_ref[:, vcol], kvc_ref[:, vcol], kvn_ref[:, vcol]], axis=0)
        kx = kvx_ref[:, kcol]
        vx = kvx_ref[:, vcol]
        for t in range(2):
            tile = 2 * g + t
            qs = _split_heads(q_ref[:, tile * LANES:(tile + 1) * LANES])
            s_loc = lax.dot_general(qs, kd, _NT, preferred_element_type=F32) + bias
            s_ctx = lax.dot_general(qs, kx, _NT, preferred_element_type=F32)
            sink = jnp.where(top, sink_ref[2 * tile], sink_ref[2 * tile + 1])
            mx = jnp.maximum(jnp.maximum(jnp.max(s_loc, axis=-1, keepdims=True),
                                         jnp.max(s_ctx, axis=-1, keepdims=True)), sink)
            p_loc = jnp.exp(s_loc - mx)
            p_ctx = jnp.exp(s_ctx - mx)
            den = (jnp.sum(p_loc, axis=-1, keepdims=True) + jnp.sum(p_ctx, axis=-1, keepdims=True)
                   + jnp.exp(sink - mx))
            o = (jnp.dot(p_loc.astype(BF16), vd, preferred_element_type=F32)
                 + jnp.dot(p_ctx.astype(BF16), vx, preferred_element_type=F32)) / den
            o_ref[:, tile * LANES:(tile + 1) * LANES] = _merge_heads(o).astype(BF16)


def _swa_attention(qkv, sinks):
    nq = SWA_XQ + SWA_CQ

    def q_block(b, j):
        return jnp.where(j < SWA_XQ, b * SWA_XQ + j, N_X // SWA_TQ + b * SWA_CQ + (j - SWA_XQ))

    def kv_block(off):
        return lambda b, j: (b * SWA_XQ + jnp.clip(j + off, 0, SWA_XQ - 1), 1)

    kv_spec = lambda off: pl.BlockSpec((SWA_TQ, KV_COLS), kv_block(off))
    return pl.pallas_call(
        _swa_kernel,
        grid=(BATCH, nq),
        in_specs=[
            pl.BlockSpec(memory_space=pltpu.SMEM),
            pl.BlockSpec((SWA_TQ, D_MODEL), lambda b, j: (q_block(b, j), 0)),
            kv_spec(-1), kv_spec(0), kv_spec(1),
            pl.BlockSpec((CTX_LEN, KV_COLS), lambda b, j: (N_X // CTX_LEN + b, 1)),
        ],
        out_specs=pl.BlockSpec((SWA_TQ, D_MODEL), lambda b, j: (q_block(b, j), 0)),
        out_shape=jax.ShapeDtypeStruct((N_ALL, D_MODEL), BF16),
        compiler_params=_cparams(("arbitrary", "arbitrary")),
        name="swa_attn",
    )(sinks, qkv, qkv, qkv, qkv, qkv)


def _prep_swa_weights(w_qkv):
    nq = B_HEADS * HEAD_DIM
    nk = B_KV_HEADS * HEAD_DIM
    wq = w_qkv[:, :nq] * (HEAD_DIM ** -0.5)
    dup = lambda w: jnp.tile(w.reshape(D_MODEL, B_KV_HEADS, 1, HEAD_DIM), (1, 1, 2, 1)).reshape(D_MODEL, -1)
    wk = dup(w_qkv[:, nq:nq + nk])
    wv = dup(w_qkv[:, nq + nk:])
    return jnp.concatenate([wq, wk, wv], axis=1).astype(BF16)


NAT_ROWS = 4
NAT_TQ = NAT_ROWS * GRID_W
NAT_XQ = SEQ // NAT_TQ
NAT_SPAN = 3 * NAT_TQ
N_HEAD_TILES = C_HEADS * HEAD_DIM // LANES
GRID_ROWS = SEQ // GRID_W


def _nat_kernel(q_ref, kp_ref, kc_ref, kn_ref, vp_ref, vc_ref, vn_ref, kx_ref, vx_ref, bias_ref, o_ref):
    qs = _split_heads(q_ref[...])
    kd = jnp.concatenate([kp_ref[...], kc_ref[...], kn_ref[...]], axis=0)
    vd = jnp.concatenate([vp_ref[...], vc_ref[...], vn_ref[...]], axis=0)
    s_loc = lax.dot_general(qs, kd, _NT, preferred_element_type=F32) + bias_ref[...].reshape(2 * NAT_TQ, NAT_SPAN)
    s_ctx = lax.dot_general(qs, kx_ref[...], _NT, preferred_element_type=F32)
    mx = jnp.maximum(jnp.max(s_loc, axis=-1, keepdims=True), jnp.max(s_ctx, axis=-1, keepdims=True))
    p_loc = jnp.exp(s_loc - mx)
    p_ctx = jnp.exp(s_ctx - mx)
    den = jnp.sum(p_loc, axis=-1, keepdims=True) + jnp.sum(p_ctx, axis=-1, keepdims=True)
    o = (jnp.dot(p_loc.astype(BF16), vd, preferred_element_type=F32)
         + jnp.dot(p_ctx.astype(BF16), vx_ref[...], preferred_element_type=F32)) / den
    o_ref[...] = _merge_heads(o).astype(BF16)


def _nat_attention(qkv, bias):
    nq = NAT_XQ + 1

    def q_block(b, j):
        return jnp.where(j < NAT_XQ, b * NAT_XQ + j, N_X // NAT_TQ + b)

    def kv_spec(off, col0):
        return pl.BlockSpec(
            (NAT_TQ, LANES),
            lambda b, t, j: (b * NAT_XQ + jnp.clip(j + off, 0, NAT_XQ - 1), col0 + t))

    def ctx_spec(col0):
        return pl.BlockSpec((CTX_LEN, LANES), lambda b, t, j: (N_X // CTX_LEN + b, col0 + t))

    def pattern(j):
        return jnp.where(j == 0, 0, jnp.where(j == NAT_XQ - 1, 2, jnp.where(j == NAT_XQ, 3, 1)))

    k0, v0 = N_HEAD_TILES, 2 * N_HEAD_TILES
    return pl.pallas_call(
        _nat_kernel,
        grid=(BATCH, N_HEAD_TILES, nq),
        in_specs=[
            pl.BlockSpec((NAT_TQ, LANES), lambda b, t, j: (q_block(b, j), t)),
            kv_spec(-1, k0), kv_spec(0, k0), kv_spec(1, k0),
            kv_spec(-1, v0), kv_spec(0, v0), kv_spec(1, v0),
            ctx_spec(k0), ctx_spec(v0),
            pl.BlockSpec((None, 2, NAT_TQ, NAT_SPAN), lambda b, t, j: (pattern(j), t, 0, 0)),
        ],
        out_specs=pl.BlockSpec((NAT_TQ, LANES), lambda b, t, j: (q_block(b, j), t)),
        out_shape=jax.ShapeDtypeStruct((N_ALL, D_MODEL), BF16),
        compiler_params=_cparams(("arbitrary", "arbitrary", "arbitrary")),
        name="nat_attn",
    )(qkv, qkv, qkv, qkv, qkv, qkv, qkv, qkv, qkv, bias)


def _nat_bias_tables(rpb):
    qr = np.arange(NAT_TQ) // GRID_W
    qc = np.arange(NAT_TQ) % GRID_W
    kr = np.arange(NAT_SPAN) // GRID_W - NAT_ROWS
    kc = np.arange(NAT_SPAN) % GRID_W
    col_start = np.clip(qc - WIN_W // 2, 0, GRID_W - WIN_W)
    col_ok = (kc[None, :] >= col_start[:, None]) & (kc[None, :] < col_start[:, None] + WIN_W)
    col_off = np.clip(kc[None, :] - qc[:, None] + (WIN_W - 1), 0, 2 * WIN_W - 2)
    row_offs, oks = [], []
    for r0 in (0, NAT_ROWS, GRID_ROWS - NAT_ROWS):
        r = r0 + qr
        rs = np.clip(r - WIN_H // 2, 0, GRID_ROWS - WIN_H)
        k_abs = r0 + kr
        row_ok = (k_abs[None, :] >= rs[:, None]) & (k_abs[None, :] < rs[:, None] + WIN_H)
        row_offs.append(np.clip(k_abs[None, :] - r[:, None] + (WIN_H - 1), 0, 2 * WIN_H - 2))
        oks.append(row_ok & col_ok)
    row_off = np.stack(row_offs)
    ok = np.stack(oks)
    vals = rpb[:, row_off, np.broadcast_to(col_off, row_off.shape)]
    tab = jnp.where(ok[None], vals, NEG).transpose(1, 0, 2, 3)
    return jnp.concatenate([tab, jnp.full((1,) + tab.shape[1:], NEG, F32)], axis=0)


def _prep_nat_weights(w_qkv):
    nq = C_HEADS * HEAD_DIM
    return jnp.concatenate([w_qkv[:, :nq] * (HEAD_DIM ** -0.5), w_qkv[:, nq:]], axis=1).astype(BF16)


def _pad_rows(a, rows=SUBLANES):
    return jnp.pad(a, ((0, rows - a.shape[0]), (0, 0)))


def kernel(x, c, ctx, c_ctx, w_ada, b_ada, norm_g, w_ffn_gu, w_ffn_down, a_w_in, a_conv_w, a_conv_b,
           a_gate_w, a_gate_b, a_lambda, a_w_out, b_w_qkv, b_sinks, b_w_o, c_w_qkv, c_rpb, c_w_o, final_g):
    xs = jnp.concatenate([x.reshape(N_X, D_MODEL), ctx.reshape(N_CTX, D_MODEL)], axis=0)
    cvec = _pad_rows(jnp.concatenate([c, c_ctx[None, :]], axis=0))
    mods = _ada_mods(cvec, w_ada, b_ada)
    rope = _rope_tables()
    out = None
    for i in range(DEPTH):
        kind, j = i % N_MIXERS, i // N_MIXERS
        last = i == DEPTH - 1
        ng = _pad_rows(norm_g[i])
        wgu0, wd0 = _prep_ffn_weights(w_ffn_gu[i, 0], w_ffn_down[i, 0])
        wgu1, wd1 = _prep_ffn_weights(w_ffn_gu[i, 1], w_ffn_down[i, 1])
        xs = _ffn(xs, mods[i], ng, wgu0, wd0, 0)
        if kind == 0:
            u = _nmm(xs, mods[i], ng, a_w_in[j].astype(BF16), F32)
            cw = _pad_rows(jnp.concatenate([a_conv_w[j], a_conv_b[j][None, :]], axis=0))
            hs = []
            for d in range(2):
                gb = _pad_rows(jnp.concatenate([a_gate_b[j, d], a_lambda[j, d][None, :]], axis=0))
                hs.append(_lru_scan(u, cw, a_gate_w[j, d].astype(BF16), gb, reverse=bool(d)))
            xs = _proj_res(xs, mods[i], a_w_out[j].astype(BF16), lru=(hs[0], hs[1], u))
        elif kind == 1:
            qkv = _nmm(xs, mods[i], ng, _prep_swa_weights(b_w_qkv[j]), BF16, rope=rope,
                       rope_cols=B_HEADS * HEAD_DIM + B_KV_HEADS * LANES)
            o = _swa_attention(qkv, b_sinks[j])
            xs = _proj_res(xs, mods[i], b_w_o[j].astype(BF16), attn=o)
        else:
            qkv = _nmm(xs, mods[i], ng, _prep_nat_weights(c_w_qkv[j]), BF16)
            o = _nat_attention(qkv, _nat_bias_tables(c_rpb[j]))
            xs = _proj_res(xs, mods[i], c_w_o[j].astype(BF16), attn=o)
        if last:
            out = _ffn(xs, mods[i], ng, wgu1, wd1, 2, final_g=final_g)
        else:
            xs = _ffn(xs, mods[i], ng, wgu1, wd1, 2)
    return out.reshape(BATCH, SEQ, D_MODEL)
```

```python
import functools

import numpy as np
import jax
import jax.numpy as jnp
from jax import lax
from jax.experimental import pallas as pl
from jax.experimental.pallas import tpu as pltpu

D_MODEL = 1024
BATCH = 2
SEQ = 16384
DEPTH = 4
GRID_W = 64
CTX_LEN = 256
N_MIXERS = 3
EPS = 1e-6
FFN_RES = 0.5
D_FF = 2816
D_RNN = 1536
LRU_BLOCKS = 12
LRU_BW = D_RNN // LRU_BLOCKS
CONV_W = 4
LRU_C = 8.0
HEAD_DIM = 64
B_HEADS = 16
B_KV_HEADS = 4
WINDOW = 128
ROPE_BASE = 10000.0
C_HEADS = 16
WIN_H = 8
WIN_W = 16
NEG = -1e30

LANES = 128
SUBLANES = 8
VMEM_LIMIT = 56 * 1024 * 1024

N_X = BATCH * SEQ
N_CTX = BATCH * CTX_LEN
N_ALL = N_X + N_CTX
TM = 512
N_TILES = N_ALL // TM
X_TILES = N_X // TM
TILES_PER_BATCH = SEQ // TM
MOD_ROWS = 16
FF_CHUNK = 256
N_FF_CHUNKS = D_FF // FF_CHUNK

F32 = jnp.float32
BF16 = jnp.bfloat16


def _cparams(sem):
    return pltpu.CompilerParams(dimension_semantics=sem, vmem_limit_bytes=VMEM_LIMIT)


def _const_spec(shape):
    nd = len(shape)
    return pl.BlockSpec(shape, lambda *_: (0,) * nd, pipeline_mode=pl.Buffered(1))


def _tile_kind(i):
    return jnp.minimum(i // TILES_PER_BATCH, BATCH)


def _adaln(x, mod_ref, ng_ref, sub):
    g = ng_ref[sub:sub + 1, :]
    shift = mod_ref[3 * sub:3 * sub + 1, :]
    scale = mod_ref[3 * sub + 1:3 * sub + 2, :]
    y = x * lax.rsqrt(jnp.mean(x * x, axis=-1, keepdims=True) + EPS)
    return (y * g) * (1.0 + scale) + shift


ADA_COLS = 1152


def _ada_kernel(c_ref, w_ref, b_ref, o_ref):
    c = c_ref[...]
    sc = c * jax.nn.sigmoid(c)
    o_ref[...] = jnp.dot(sc, w_ref[...], preferred_element_type=F32,
                         precision=lax.Precision.HIGHEST) + b_ref[...]


def _ada_mods(cvec, w_ada, b_ada):
    out = pl.pallas_call(
        _ada_kernel,
        grid=(DEPTH, 9 * D_MODEL // ADA_COLS),
        in_specs=[
            pl.BlockSpec((SUBLANES, D_MODEL), lambda i, j: (0, 0)),
            pl.BlockSpec((None, D_MODEL, ADA_COLS), lambda i, j: (i, 0, j)),
            pl.BlockSpec((None, 1, ADA_COLS), lambda i, j: (i, 0, j)),
        ],
        out_specs=pl.BlockSpec((None, SUBLANES, ADA_COLS), lambda i, j: (i, 0, j)),
        out_shape=jax.ShapeDtypeStruct((DEPTH, SUBLANES, 9 * D_MODEL), F32),
        compiler_params=_cparams(("arbitrary", "arbitrary")),
        name="ada_mods",
    )(cvec, w_ada, b_ada.reshape(DEPTH, 1, 9 * D_MODEL))
    mods = out.reshape(DEPTH, SUBLANES, 9, D_MODEL)[:, :BATCH + 1]
    return jnp.pad(mods, ((0, 0), (0, 0), (0, MOD_ROWS - 9), (0, 0)))


def _ffn_kernel(*refs, sub, final):
    if final:
        x_ref, mod_ref, ng_ref, wgu_ref, wd_ref, fg_ref, o_ref, hb_ref, acc_ref = refs
    else:
        x_ref, mod_ref, ng_ref, wgu_ref, wd_ref, o_ref, hb_ref, acc_ref = refs
    x = x_ref[...]
    hb_ref[...] = _adaln(x, mod_ref, ng_ref, sub).astype(BF16)
    acc_ref[...] = jnp.zeros_like(acc_ref)

    for c in range(N_FF_CHUNKS):
        gu = jnp.dot(hb_ref[...], wgu_ref[c], preferred_element_type=F32)
        g = gu[:, :FF_CHUNK]
        u = gu[:, FF_CHUNK:]
        a = ((g * jax.nn.sigmoid(g)) * u).astype(BF16)
        acc_ref[...] += jnp.dot(a, wd_ref[c], preferred_element_type=F32)
    gate = mod_ref[3 * sub + 2:3 * sub + 3, :]
    y = x + (FFN_RES * gate) * acc_ref[...]
    if final:
        y = (y * lax.rsqrt(jnp.mean(y * y, axis=-1, keepdims=True) + EPS)) * fg_ref[...]
    o_ref[...] = y


def _ffn(xs, mods_i, ng_i, wgu, wd, sub, final_g=None):
    final = final_g is not None
    n_tiles = X_TILES if final else N_TILES
    in_specs = [
        pl.BlockSpec((TM, D_MODEL), lambda i: (i, 0)),
        pl.BlockSpec((None, MOD_ROWS, D_MODEL), lambda i: (_tile_kind(i), 0, 0)),
        _const_spec((SUBLANES, D_MODEL)),
        _const_spec((N_FF_CHUNKS, D_MODEL, 2 * FF_CHUNK)),
        _const_spec((N_FF_CHUNKS, FF_CHUNK, D_MODEL)),
    ]
    args = [xs, mods_i, ng_i, wgu, wd]
    if final:
        in_specs.append(_const_spec((1, D_MODEL)))
        args.append(final_g.reshape(1, D_MODEL))
    return pl.pallas_call(
        functools.partial(_ffn_kernel, sub=sub, final=final),
        grid=(n_tiles,),
        in_specs=in_specs,
        out_specs=pl.BlockSpec((TM, D_MODEL), lambda i: (i, 0)),
        out_shape=jax.ShapeDtypeStruct((n_tiles * TM, D_MODEL), F32),
        scratch_shapes=[pltpu.VMEM((TM, D_MODEL), BF16), pltpu.VMEM((TM, D_MODEL), F32)],
        compiler_params=_cparams(("arbitrary",)),
        name="ffn_final" if final else "ffn",
    )(*args)


def _prep_ffn_weights(w_gu, w_dn):
    wg = w_gu[:, :D_FF].reshape(D_MODEL, N_FF_CHUNKS, FF_CHUNK)
    wu = w_gu[:, D_FF:].reshape(D_MODEL, N_FF_CHUNKS, FF_CHUNK)
    wgu = jnp.concatenate([wg, wu], axis=-1).transpose(1, 0, 2).astype(BF16)
    wd = w_dn.reshape(N_FF_CHUNKS, FF_CHUNK, D_MODEL).astype(BF16)
    return wgu, wd


NMM_COLS = 512


def _rope_tile(y, c, s1, s2):
    return y * c + pltpu.roll(y, LANES - 16, 1) * s1 + pltpu.roll(y, 16, 1) * s2


def _nmm_kernel(*refs, n_cols, rope_cols, out_dtype):
    if rope_cols:
        x_ref, mod_ref, ng_ref, w_ref, c_ref, s1_ref, s2_ref, o_ref = refs
    else:
        x_ref, mod_ref, ng_ref, w_ref, o_ref = refs
    hb = _adaln(x_ref[...], mod_ref, ng_ref, 1).astype(BF16)
    for c0 in range(0, n_cols, NMM_COLS):
        y = jnp.dot(hb, w_ref[:, c0:c0 + NMM_COLS], preferred_element_type=F32)
        if c0 < rope_cols:
            c, s1, s2 = c_ref[...], s1_ref[...], s2_ref[...]
            y = jnp.concatenate(
                [_rope_tile(y[:, t:t + LANES], c, s1, s2) for t in range(0, NMM_COLS, LANES)], axis=1)
        o_ref[:, c0:c0 + NMM_COLS] = y.astype(out_dtype)


def _nmm(xs, mods_i, ng_i, w, out_dtype, rope=None, rope_cols=0):
    n_cols = w.shape[1]
    in_specs = [
        pl.BlockSpec((TM, D_MODEL), lambda i: (i, 0)),
        pl.BlockSpec((None, MOD_ROWS, D_MODEL), lambda i: (_tile_kind(i), 0, 0)),
        _const_spec((SUBLANES, D_MODEL)),
        _const_spec((D_MODEL, n_cols)),
    ]
    args = [xs, mods_i, ng_i, w]
    if rope_cols:
        rmap = lambda i: (jnp.where(i < X_TILES, i % TILES_PER_BATCH, TILES_PER_BATCH), 0)
        in_specs += [pl.BlockSpec((TM, LANES), rmap)] * 3
        args += list(rope)
    return pl.pallas_call(
        functools.partial(_nmm_kernel, n_cols=n_cols, rope_cols=rope_cols, out_dtype=out_dtype),
        grid=(N_TILES,),
        in_specs=in_specs,
        out_specs=pl.BlockSpec((TM, n_cols), lambda i: (i, 0)),
        out_shape=jax.ShapeDtypeStruct((N_ALL, n_cols), out_dtype),
        compiler_params=_cparams(("arbitrary",)),
        name="adaln_proj",
    )(*args)


def _rope_tables():
    quarter = HEAD_DIM // 4
    pos = jnp.arange(SEQ)
    inv = ROPE_BASE ** (-jnp.arange(quarter, dtype=F32) / quarter)
    lane = np.arange(LANES)
    d = lane % HEAD_DIM
    use_col = (d >= HEAD_DIM // 2)
    first = (d % (HEAD_DIM // 2)) < quarter
    ang_row = (pos // GRID_W).astype(F32)[:, None] * inv
    ang_col = (pos % GRID_W).astype(F32)[:, None] * inv
    fi = d % quarter
    ang = jnp.where(use_col[None, :], ang_col[:, fi], ang_row[:, fi])
    cos, sin = jnp.cos(ang), jnp.sin(ang)
    s1 = jnp.where(first[None, :], -sin, 0.0)
    s2 = jnp.where(first[None, :], 0.0, sin)
    ident = jnp.ones((TM, LANES), F32)
    zero = jnp.zeros((TM, LANES), F32)
    return (jnp.concatenate([cos, ident]), jnp.concatenate([s1, zero]), jnp.concatenate([s2, zero]))


def _proj_res_attn_kernel(x_ref, a_ref, mod_ref, w_ref, o_ref):
    y = jnp.dot(a_ref[...], w_ref[...], preferred_element_type=F32)
    o_ref[...] = x_ref[...] + mod_ref[5:6, :] * y


def _gelu_tanh(g):
    return 0.5 * g * (1.0 + jnp.tanh(np.sqrt(2.0 / np.pi).astype(np.float32) * (g + 0.044715 * (g * g * g))))


def _proj_res_lru_kernel(x_ref, hf_ref, hb_ref, g_ref, mod_ref, w_ref, o_ref):
    h = jnp.concatenate([hf_ref[n] + hb_ref[n] for n in range(LRU_BLOCKS)], axis=1)
    a = (h * _gelu_tanh(g_ref[...])).astype(BF16)
    y = jnp.dot(a, w_ref[...], preferred_element_type=F32)
    o_ref[...] = x_ref[...] + mod_ref[5:6, :] * y


def _proj_res(xs, mods_i, w, attn=None, lru=None):
    x_spec = pl.BlockSpec((TM, D_MODEL), lambda i: (i, 0))
    mod_spec = pl.BlockSpec((None, MOD_ROWS, D_MODEL), lambda i: (_tile_kind(i), 0, 0))
    if attn is not None:
        kern = _proj_res_attn_kernel
        in_specs = [x_spec, pl.BlockSpec((TM, D_MODEL), lambda i: (i, 0)), mod_spec,
                    _const_spec((D_MODEL, D_MODEL))]
        args = [xs, attn, mods_i, w]
    else:
        hf, hb, g = lru
        kern = _proj_res_lru_kernel
        h_spec = pl.BlockSpec((LRU_BLOCKS, TM, LRU_BW), lambda i: (0, i, 0))
        in_specs = [x_spec, h_spec, h_spec, pl.BlockSpec((TM, D_RNN), lambda i: (i, 0)), mod_spec,
                    _const_spec((D_RNN, D_MODEL))]
        args = [xs, hf, hb, g, mods_i, w]
    return pl.pallas_call(
        kern,
        grid=(N_TILES,),
        in_specs=in_specs,
        out_specs=pl.BlockSpec((TM, D_MODEL), lambda i: (i, 0)),
        out_shape=jax.ShapeDtypeStruct((N_ALL, D_MODEL), F32),
        compiler_params=_cparams(("arbitrary",)),
        name="proj_res",
    )(*args)


def _lru_in_kernel(x_ref, mod_ref, ng_ref, w_ref, g_ref, xr_ref):
    hb = _adaln(x_ref[...], mod_ref, ng_ref, 1).astype(BF16)
    for c0 in range(0, D_RNN, NMM_COLS):
        g_ref[:, c0:c0 + NMM_COLS] = jnp.dot(hb, w_ref[:, c0:c0 + NMM_COLS], preferred_element_type=F32)
    for c0 in range(0, D_RNN, NMM_COLS):
        y = jnp.dot(hb, w_ref[:, D_RNN + c0:D_RNN + c0 + NMM_COLS], preferred_element_type=F32)
        for t in range(NMM_COLS // LRU_BW):
            xr_ref[c0 // LRU_BW + t] = y[:, t * LRU_BW:(t + 1) * LRU_BW]


def _lru_in_proj(xs, mods_i, ng_i, w):
    return pl.pallas_call(
        _lru_in_kernel,
        grid=(N_TILES,),
        in_specs=[
            pl.BlockSpec((TM, D_MODEL), lambda i: (i, 0)),
            pl.BlockSpec((None, MOD_ROWS, D_MODEL), lambda i: (_tile_kind(i), 0, 0)),
            _const_spec((SUBLANES, D_MODEL)),
            _const_spec((D_MODEL, 2 * D_RNN)),
        ],
        out_specs=[pl.BlockSpec((TM, D_RNN), lambda i: (i, 0)),
                   pl.BlockSpec((LRU_BLOCKS, TM, LRU_BW), lambda i: (0, i, 0))],
        out_shape=[jax.ShapeDtypeStruct((N_ALL, D_RNN), F32),
                   jax.ShapeDtypeStruct((LRU_BLOCKS, N_ALL, LRU_BW), F32)],
        compiler_params=_cparams(("arbitrary",)),
        name="lru_in_proj",
    )(xs, mods_i, ng_i, w)


LRU_T = 256
LRU_SEG = LRU_T // SUBLANES
X_CHUNKS = SEQ // LRU_T
HALO = SUBLANES


def _softplus(x):
    return jnp.maximum(x, 0.0) + jnp.log1p(jnp.exp(-jnp.abs(x)))


def _sigmoid(x):
    return 0.5 * jnp.tanh(0.5 * x) + 0.5


def _lru_kernel(cur_ref, prev_ref, next_ref, cw_ref, gw_ref, gb_ref, o_ref,
                h_ref, p_ref, carry_ref, *, reverse):
    m = pl.program_id(1)
    is_x = m > 0
    c = (X_CHUNKS - m) if reverse else (m - 1)

    @pl.when(m == 0)
    def _():
        carry_ref[...] = jnp.zeros_like(carry_ref)

    has_prev = jnp.logical_and(is_x, c > 0)
    has_next = jnp.logical_and(is_x, c < X_CHUNKS - 1)
    sub = lax.broadcasted_iota(jnp.int32, (SUBLANES, LRU_BW), 0)
    steps = range(LRU_SEG - 1, -1, -1) if reverse else range(LRU_SEG)
    segs = range(SUBLANES - 1, -1, -1) if reverse else range(SUBLANES)

    for n in range(LRU_BLOCKS):
        cols = slice(n * LRU_BW, (n + 1) * LRU_BW)
        xg = [cur_ref[n, pl.ds(k, SUBLANES, stride=LRU_SEG), :] for k in range(LRU_SEG)]
        before = jnp.where(has_prev, prev_ref[n, HALO - 1:HALO, :], 0.0)
        after0 = jnp.where(has_next, next_ref[n, 0:1, :], 0.0)
        after1 = jnp.where(has_next, next_ref[n, 1:2, :], 0.0)
        xm1 = jnp.where(sub == 0, before, pltpu.roll(xg[LRU_SEG - 1], 1, 0))
        xp0 = jnp.where(sub == SUBLANES - 1, after0, pltpu.roll(xg[0], SUBLANES - 1, 0))
        xp1 = jnp.where(sub == SUBLANES - 1, after1, pltpu.roll(xg[1], SUBLANES - 1, 0))
        xe = [xm1] + xg + [xp0, xp1]
        cb = cw_ref[CONV_W:CONV_W + 1, cols]
        taps = [cw_ref[j:j + 1, cols] for j in range(CONV_W)]
        xc = jnp.concatenate(
            [cb + xe[k] * taps[0] + xe[k + 1] * taps[1] + xe[k + 2] * taps[2] + xe[k + 3] * taps[3]
             for k in range(LRU_SEG)], axis=0)
        ri = jnp.dot(xc.astype(BF16), gw_ref[n], preferred_element_type=F32)
        r = _sigmoid(ri[:, :LRU_BW] + gb_ref[0:1, cols])
        i = _sigmoid(ri[:, LRU_BW:] + gb_ref[1:2, cols])
        log_a = (-LRU_C * r) * _softplus(-gb_ref[2:3, cols])
        a = jnp.exp(log_a)
        t = jnp.tanh(log_a)
        b = jnp.sqrt((-2.0 * t) / (1.0 - t)) * (i * xc)

        h = jnp.zeros((SUBLANES, LRU_BW), F32)
        p = jnp.ones((SUBLANES, LRU_BW), F32)
        for k in steps:
            grp = slice(k * SUBLANES, (k + 1) * SUBLANES)
            h = a[grp] * h + b[grp]
            p = a[grp] * p
            h_ref[n, grp, :] = h
            p_ref[n, grp, :] = p
        st = carry_ref[n, 0:1, :]
        enter = jnp.zeros((SUBLANES, LRU_BW), F32)
        for s in segs:
            enter = jnp.where(sub == s, st, enter)
            st = p[s:s + 1, :] * st + h[s:s + 1, :]
        carry_ref[n, 0:1, :] = st
        for k in range(LRU_SEG):
            grp = slice(k * SUBLANES, (k + 1) * SUBLANES)
            o_ref[n, pl.ds(k, SUBLANES, stride=LRU_SEG), :] = h_ref[n, grp, :] + p_ref[n, grp, :] * enter


def _lru_scan(xr, cw, gw, gb, reverse):
    blocks_per_chunk = LRU_T // HALO
    n_halo_blocks = N_ALL // HALO

    def row_block(b, m):
        c = (X_CHUNKS - m) if reverse else (m - 1)
        return jnp.where(m == 0, N_X // LRU_T + b, b * X_CHUNKS + c)

    cur = lambda b, m: (0, row_block(b, m), 0)
    prev = lambda b, m: (0, jnp.maximum(row_block(b, m) * blocks_per_chunk - 1, 0), 0)
    nxt = lambda b, m: (0, jnp.minimum((row_block(b, m) + 1) * blocks_per_chunk, n_halo_blocks - 1), 0)
    return pl.pallas_call(
        functools.partial(_lru_kernel, reverse=reverse),
        grid=(BATCH, X_CHUNKS + 1),
        in_specs=[
            pl.BlockSpec((LRU_BLOCKS, LRU_T, LRU_BW), cur),
            pl.BlockSpec((LRU_BLOCKS, HALO, LRU_BW), prev),
            pl.BlockSpec((LRU_BLOCKS, HALO, LRU_BW), nxt),
            _const_spec((SUBLANES, D_RNN)),
            _const_spec((LRU_BLOCKS, LRU_BW, 2 * LRU_BW)),
            _const_spec((SUBLANES, D_RNN)),
        ],
        out_specs=pl.BlockSpec((LRU_BLOCKS, LRU_T, LRU_BW), cur),
        out_shape=jax.ShapeDtypeStruct((LRU_BLOCKS, N_ALL, LRU_BW), F32),
        scratch_shapes=[
            pltpu.VMEM((LRU_BLOCKS, LRU_T, LRU_BW), F32),
            pltpu.VMEM((LRU_BLOCKS, LRU_T, LRU_BW), F32),
            pltpu.VMEM((LRU_BLOCKS, SUBLANES, LRU_BW), F32),
        ],
        compiler_params=_cparams(("arbitrary", "arbitrary")),
        name="lru_bwd" if reverse else "lru_fwd",
    )(xr, xr, xr, cw, gw, gb)


SWA_TQ = 128
SWA_XQ = SEQ // SWA_TQ
SWA_CQ = CTX_LEN // SWA_TQ
SWA_SPAN = 3 * SWA_TQ
KV_COLS = 2 * B_KV_HEADS * LANES
_NT = (((1,), (1,)), ((), ()))


def _split_heads(qt):
    lo = lax.broadcasted_iota(jnp.int32, qt.shape, 1) < HEAD_DIM
    zero = jnp.zeros_like(qt)
    return jnp.concatenate([jnp.where(lo, qt, zero), jnp.where(lo, zero, qt)], axis=0)


def _merge_heads(o):
    t = o.shape[0] // 2
    lo = lax.broadcasted_iota(jnp.int32, (t, LANES), 1) < HEAD_DIM
    return jnp.where(lo, o[:t], o[t:])


def _swa_kernel(sink_ref, q_ref, kvp_ref, kvc_ref, kvn_ref, kvx_ref, o_ref):
    j = pl.program_id(1)
    iq = lax.broadcasted_iota(jnp.int32, (SWA_TQ, SWA_SPAN), 0)
    ik = lax.broadcasted_iota(jnp.int32, (SWA_TQ, SWA_SPAN), 1)
    lo_ok = jnp.where(j > 0, 0, SWA_TQ)
    hi_ok = jnp.where(j < SWA_XQ - 1, SWA_SPAN, 2 * SWA_TQ)
    hi_ok = jnp.where(j < SWA_XQ, hi_ok, 0)
    dist = iq + WINDOW - ik
    bias1 = jnp.where(jnp.abs(dist) <= WINDOW, 0.0, NEG)
    bias1 = jnp.where(ik >= lo_ok, bias1, NEG)
    bias1 = jnp.where(ik < hi_ok, bias1, NEG)
    bias = jnp.concatenate([bias1, bias1], axis=0)
    top = lax.broadcasted_iota(jnp.int32, (2 * SWA_TQ, 1), 0) < SWA_TQ
    for g in range(B_KV_HEADS):
        kcol = slice(g * LANES, (g + 1) * LANES)
        vcol = slice((B_KV_HEADS + g) * LANES, (B_KV_HEADS + g + 1) * LANES)
        kd = jnp.concatenate([kvp_ref[:, kcol], kvc_ref[:, kcol], kvn_ref[:, kcol]], axis=0)
        vd = jnp.concatenate([kvp_ref[:, vcol], kvc_ref[:, vcol], kvn_ref[:, vcol]], axis=0)
        kx = kvx_ref[:, kcol]
        vx = kvx_ref[:, vcol]
        for t in range(2):
            tile = 2 * g + t
            qs = _split_heads(q_ref[:, tile * LANES:(tile + 1) * LANES])
            s_loc = lax.dot_general(qs, kd, _NT, preferred_element_type=F32) + bias
            s_ctx = lax.dot_general(qs, kx, _NT, preferred_element_type=F32)
            sink = jnp.where(top, sink_ref[2 * tile], sink_ref[2 * tile + 1])
            mx = jnp.maximum(jnp.maximum(jnp.max(s_loc, axis=-1, keepdims=True),
                                         jnp.max(s_ctx, axis=-1, keepdims=True)), sink)
            p_loc = jnp.exp(s_loc - mx)
            p_ctx = jnp.exp(s_ctx - mx)
            den = (jnp.sum(p_loc, axis=-1, keepdims=True) + jnp.sum(p_ctx, axis=-1, keepdims=True)
                   + jnp.exp(sink - mx))
            o = (jnp.dot(p_loc.astype(BF16), vd, preferred_element_type=F32)
                 + jnp.dot(p_ctx.astype(BF16), vx, preferred_element_type=F32)) / den
            o_ref[:, tile * LANES:(tile + 1) * LANES] = _merge_heads(o).astype(BF16)


def _swa_attention(qkv, sinks):
    nq = SWA_XQ + SWA_CQ

    def q_block(b, j):
        return jnp.where(j < SWA_XQ, b * SWA_XQ + j, N_X // SWA_TQ + b * SWA_CQ + (j - SWA_XQ))

    def kv_block(off):
        return lambda b, j: (b * SWA_XQ + jnp.clip(j + off, 0, SWA_XQ - 1), 1)

    kv_spec = lambda off: pl.BlockSpec((SWA_TQ, KV_COLS), kv_block(off))
    return pl.pallas_call(
        _swa_kernel,
        grid=(BATCH, nq),
        in_specs=[
            pl.BlockSpec(memory_space=pltpu.SMEM),
            pl.BlockSpec((SWA_TQ, D_MODEL), lambda b, j: (q_block(b, j), 0)),
            kv_spec(-1), kv_spec(0), kv_spec(1),
            pl.BlockSpec((CTX_LEN, KV_COLS), lambda b, j: (N_X // CTX_LEN + b, 1)),
        ],
        out_specs=pl.BlockSpec((SWA_TQ, D_MODEL), lambda b, j: (q_block(b, j), 0)),
        out_shape=jax.ShapeDtypeStruct((N_ALL, D_MODEL), BF16),
        compiler_params=_cparams(("arbitrary", "arbitrary")),
        name="swa_attn",
    )(sinks, qkv, qkv, qkv, qkv, qkv)


def _prep_swa_weights(w_qkv):
    nq = B_HEADS * HEAD_DIM
    nk = B_KV_HEADS * HEAD_DIM
    wq = w_qkv[:, :nq] * (HEAD_DIM ** -0.5)
    dup = lambda w: jnp.tile(w.reshape(D_MODEL, B_KV_HEADS, 1, HEAD_DIM), (1, 1, 2, 1)).reshape(D_MODEL, -1)
    wk = dup(w_qkv[:, nq:nq + nk])
    wv = dup(w_qkv[:, nq + nk:])
    return jnp.concatenate([wq, wk, wv], axis=1).astype(BF16)


NAT_ROWS = 4
NAT_TQ = NAT_ROWS * GRID_W
NAT_XQ = SEQ // NAT_TQ
NAT_SPAN = 3 * NAT_TQ
N_HEAD_TILES = C_HEADS * HEAD_DIM // LANES
GRID_ROWS = SEQ // GRID_W


def _nat_kernel(q_ref, kp_ref, kc_ref, kn_ref, vp_ref, vc_ref, vn_ref, kx_ref, vx_ref, bias_ref, o_ref):
    qs = _split_heads(q_ref[...])
    kd = jnp.concatenate([kp_ref[...], kc_ref[...], kn_ref[...]], axis=0)
    vd = jnp.concatenate([vp_ref[...], vc_ref[...], vn_ref[...]], axis=0)
    s_loc = lax.dot_general(qs, kd, _NT, preferred_element_type=F32) + bias_ref[...].reshape(2 * NAT_TQ, NAT_SPAN)
    s_ctx = lax.dot_general(qs, kx_ref[...], _NT, preferred_element_type=F32)
    mx = jnp.maximum(jnp.max(s_loc, axis=-1, keepdims=True), jnp.max(s_ctx, axis=-1, keepdims=True))
    p_loc = jnp.exp(s_loc - mx)
    p_ctx = jnp.exp(s_ctx - mx)
    den = jnp.sum(p_loc, axis=-1, keepdims=True) + jnp.sum(p_ctx, axis=-1, keepdims=True)
    o = (jnp.dot(p_loc.astype(BF16), vd, preferred_element_type=F32)
         + jnp.dot(p_ctx.astype(BF16), vx_ref[...], preferred_element_type=F32)) / den
    o_ref[...] = _merge_heads(o).astype(BF16)


def _nat_attention(qkv, bias):
    nq = NAT_XQ + 1

    def q_block(b, j):
        return jnp.where(j < NAT_XQ, b * NAT_XQ + j, N_X // NAT_TQ + b)

    def kv_spec(off, col0):
        return pl.BlockSpec(
            (NAT_TQ, LANES),
            lambda b, t, j: (b * NAT_XQ + jnp.clip(j + off, 0, NAT_XQ - 1), col0 + t))

    def ctx_spec(col0):
        return pl.BlockSpec((CTX_LEN, LANES), lambda b, t, j: (N_X // CTX_LEN + b, col0 + t))

    def pattern(j):
        return jnp.where(j == 0, 0, jnp.where(j == NAT_XQ - 1, 2, jnp.where(j == NAT_XQ, 3, 1)))

    k0, v0 = N_HEAD_TILES, 2 * N_HEAD_TILES
    return pl.pallas_call(
        _nat_kernel,
        grid=(BATCH, N_HEAD_TILES, nq),
        in_specs=[
            pl.BlockSpec((NAT_TQ, LANES), lambda b, t, j: (q_block(b, j), t)),
            kv_spec(-1, k0), kv_spec(0, k0), kv_spec(1, k0),
            kv_spec(-1, v0), kv_spec(0, v0), kv_spec(1, v0),
            ctx_spec(k0), ctx_spec(v0),
            pl.BlockSpec((None, 2, NAT_TQ, NAT_SPAN), lambda b, t, j: (pattern(j), t, 0, 0)),
        ],
        out_specs=pl.BlockSpec((NAT_TQ, LANES), lambda b, t, j: (q_block(b, j), t)),
        out_shape=jax.ShapeDtypeStruct((N_ALL, D_MODEL), BF16),
        compiler_params=_cparams(("arbitrary", "arbitrary", "arbitrary")),
        name="nat_attn",
    )(qkv, qkv, qkv, qkv, qkv, qkv, qkv, qkv, qkv, bias)


def _nat_bias_tables(rpb):
    n_rpb_rows = 2 * WIN_H - 1
    key_rows = NAT_SPAN // GRID_W
    col = np.arange(GRID_W)
    col_start = np.clip(col - WIN_W // 2, 0, GRID_W - WIN_W)
    col_ok = (col[None, :] >= col_start[:, None]) & (col[None, :] < col_start[:, None] + WIN_W)
    col_off = np.clip(col[None, :] - col[:, None] + (WIN_W - 1), 0, 2 * WIN_W - 2)
    by_col = jnp.where(col_ok[None, None], rpb[:, :, col_off], NEG)
    by_col = jnp.concatenate([by_col, jnp.full((C_HEADS, 1, GRID_W, GRID_W), NEG, F32)], axis=1)
    qr = np.arange(NAT_ROWS)
    kr = np.arange(key_rows) - NAT_ROWS
    row_sel = []
    for r0 in (0, NAT_ROWS, GRID_ROWS - NAT_ROWS):
        r = r0 + qr
        rs = np.clip(r - WIN_H // 2, 0, GRID_ROWS - WIN_H)
        k_abs = r0 + kr
        row_ok = (k_abs[None, :] >= rs[:, None]) & (k_abs[None, :] < rs[:, None] + WIN_H)
        row_sel.append(np.where(row_ok, k_abs[None, :] - r[:, None] + (WIN_H - 1), n_rpb_rows))
    row_sel.append(np.full((NAT_ROWS, key_rows), n_rpb_rows))
    row_sel = np.stack(row_sel).reshape(-1)
    tab = jnp.take(by_col, row_sel, axis=1)
    tab = tab.reshape(C_HEADS, 4, NAT_ROWS, key_rows, GRID_W, GRID_W)
    return tab.transpose(1, 0, 2, 4, 3, 5).reshape(4, C_HEADS, NAT_TQ, NAT_SPAN)


def _prep_nat_weights(w_qkv):
    nq = C_HEADS * HEAD_DIM
    return jnp.concatenate([w_qkv[:, :nq] * (HEAD_DIM ** -0.5), w_qkv[:, nq:]], axis=1).astype(BF16)


def _pad_rows(a, rows=SUBLANES):
    return jnp.pad(a, ((0, rows - a.shape[0]), (0, 0)))


def kernel(x, c, ctx, c_ctx, w_ada, b_ada, norm_g, w_ffn_gu, w_ffn_down, a_w_in, a_conv_w, a_conv_b,
           a_gate_w, a_gate_b, a_lambda, a_w_out, b_w_qkv, b_sinks, b_w_o, c_w_qkv, c_rpb, c_w_o, final_g):
    xs = jnp.concatenate([x.reshape(N_X, D_MODEL), ctx.reshape(N_CTX, D_MODEL)], axis=0)
    cvec = _pad_rows(jnp.concatenate([c, c_ctx[None, :]], axis=0))
    mods = _ada_mods(cvec, w_ada, b_ada)
    rope = _rope_tables()
    out = None
    for i in range(DEPTH):
        kind, j = i % N_MIXERS, i // N_MIXERS
        last = i == DEPTH - 1
        ng = _pad_rows(norm_g[i])
        wgu0, wd0 = _prep_ffn_weights(w_ffn_gu[i, 0], w_ffn_down[i, 0])
        wgu1, wd1 = _prep_ffn_weights(w_ffn_gu[i, 1], w_ffn_down[i, 1])
        xs = _ffn(xs, mods[i], ng, wgu0, wd0, 0)
        if kind == 0:
            g_branch, xr = _lru_in_proj(xs, mods[i], ng, a_w_in[j].astype(BF16))
            cw = _pad_rows(jnp.concatenate([a_conv_w[j], a_conv_b[j][None, :]], axis=0))
            hs = []
            for d in range(2):
                gb = _pad_rows(jnp.concatenate([a_gate_b[j, d], a_lambda[j, d][None, :]], axis=0))
                gw = jnp.concatenate([a_gate_w[j, d, 0], a_gate_w[j, d, 1]], axis=-1).astype(BF16)
                hs.append(_lru_scan(xr, cw, gw, gb, reverse=bool(d)))
            xs = _proj_res(xs, mods[i], a_w_out[j].astype(BF16), lru=(hs[0], hs[1], g_branch))
        elif kind == 1:
            qkv = _nmm(xs, mods[i], ng, _prep_swa_weights(b_w_qkv[j]), BF16, rope=rope,
                       rope_cols=B_HEADS * HEAD_DIM + B_KV_HEADS * LANES)
            o = _swa_attention(qkv, b_sinks[j])
            xs = _proj_res(xs, mods[i], b_w_o[j].astype(BF16), attn=o)
        else:
            qkv = _nmm(xs, mods[i], ng, _prep_nat_weights(c_w_qkv[j]), BF16)
            o = _nat_attention(qkv, _nat_bias_tables(c_rpb[j]))
            xs = _proj_res(xs, mods[i], c_w_o[j].astype(BF16), attn=o)
        if last:
            out = _ffn(xs, mods[i], ng, wgu1, wd1, 2, final_g=final_g)
        else:
            xs = _ffn(xs, mods[i], ng, wgu1, wd1, 2)
    return out.reshape(BATCH, SEQ, D_MODEL)
```

```python
import functools

import numpy as np
import jax
import jax.numpy as jnp
from jax import lax
from jax.experimental import pallas as pl
from jax.experimental.pallas import tpu as pltpu

D_MODEL = 1024
BATCH = 2
SEQ = 16384
DEPTH = 4
GRID_W = 64
CTX_LEN = 256
N_MIXERS = 3
EPS = 1e-6
FFN_RES = 0.5
D_FF = 2816
D_RNN = 1536
LRU_BLOCKS = 12
LRU_BW = D_RNN // LRU_BLOCKS
CONV_W = 4
LRU_C = 8.0
HEAD_DIM = 64
B_HEADS = 16
B_KV_HEADS = 4
WINDOW = 128
ROPE_BASE = 10000.0
C_HEADS = 16
WIN_H = 8
WIN_W = 16
NEG = -1e30

LANES = 128
SUBLANES = 8
VMEM_LIMIT = 56 * 1024 * 1024

N_X = BATCH * SEQ
N_CTX = BATCH * CTX_LEN
N_ALL = N_X + N_CTX
TM = 512
N_TILES = N_ALL // TM
X_TILES = N_X // TM
TILES_PER_BATCH = SEQ // TM
MOD_ROWS = 16
FF_CHUNK = 256
N_FF_CHUNKS = D_FF // FF_CHUNK

F32 = jnp.float32
BF16 = jnp.bfloat16


def _cparams(sem):
    return pltpu.CompilerParams(dimension_semantics=sem, vmem_limit_bytes=VMEM_LIMIT)


def _const_spec(shape):
    nd = len(shape)
    return pl.BlockSpec(shape, lambda *_: (0,) * nd, pipeline_mode=pl.Buffered(1))


def _tile_kind(i):
    return jnp.minimum(i // TILES_PER_BATCH, BATCH)


def _adaln(x, mod_ref, ng_ref, sub):
    g = ng_ref[sub:sub + 1, :]
    shift = mod_ref[3 * sub:3 * sub + 1, :]
    scale = mod_ref[3 * sub + 1:3 * sub + 2, :]
    y = x * lax.rsqrt(jnp.mean(x * x, axis=-1, keepdims=True) + EPS)
    return (y * g) * (1.0 + scale) + shift


ADA_COLS = 1152


def _ada_kernel(c_ref, w_ref, b_ref, o_ref):
    c = c_ref[...]
    sc = c * jax.nn.sigmoid(c)
    o_ref[...] = jnp.dot(sc, w_ref[...], preferred_element_type=F32,
                         precision=lax.Precision.HIGHEST) + b_ref[...]


def _ada_mods(cvec, w_ada, b_ada):
    out = pl.pallas_call(
        _ada_kernel,
        grid=(DEPTH, 9 * D_MODEL // ADA_COLS),
        in_specs=[
            pl.BlockSpec((SUBLANES, D_MODEL), lambda i, j: (0, 0)),
            pl.BlockSpec((None, D_MODEL, ADA_COLS), lambda i, j: (i, 0, j)),
            pl.BlockSpec((None, 1, ADA_COLS), lambda i, j: (i, 0, j)),
        ],
        out_specs=pl.BlockSpec((None, SUBLANES, ADA_COLS), lambda i, j: (i, 0, j)),
        out_shape=jax.ShapeDtypeStruct((DEPTH, SUBLANES, 9 * D_MODEL), F32),
        compiler_params=_cparams(("arbitrary", "arbitrary")),
        name="ada_mods",
    )(cvec, w_ada, b_ada.reshape(DEPTH, 1, 9 * D_MODEL))
    mods = out.reshape(DEPTH, SUBLANES, 9, D_MODEL)[:, :BATCH + 1]
    return jnp.pad(mods, ((0, 0), (0, 0), (0, MOD_ROWS - 9), (0, 0)))


def _gelu_tanh(g):
    return 0.5 * g * (1.0 + jnp.tanh(np.sqrt(2.0 / np.pi).astype(np.float32) * (g + 0.044715 * (g * g * g))))


def _ffn_kernel(*refs, sub, mixer, final):
    x_ref, mod_ref, ng_ref, wgu_ref, wd_ref = refs[:5]
    rest = list(refs[5:])
    if mixer == "attn":
        a_ref, wo_ref = rest[:2]
        rest = rest[2:]
        mixed = a_ref[...]
    elif mixer == "lru":
        hf_ref, hbk_ref, g_ref, wo_ref = rest[:4]
        rest = rest[4:]
        h = jnp.concatenate([hf_ref[n] + hbk_ref[n] for n in range(LRU_BLOCKS)], axis=1)
        mixed = (h * _gelu_tanh(g_ref[...])).astype(BF16)
    fg_ref = rest.pop(0) if final else None
    o_ref, hb_ref, acc_ref, xm_ref = rest
    if mixer is None:
        xm_ref = x_ref
    else:
        y = jnp.dot(mixed, wo_ref[...], preferred_element_type=F32)
        xm_ref[...] = x_ref[...] + mod_ref[5:6, :] * y
    hb_ref[...] = _adaln(xm_ref[...], mod_ref, ng_ref, sub).astype(BF16)
    acc_ref[...] = jnp.zeros_like(acc_ref)

    for c in range(N_FF_CHUNKS):
        lo = c * FF_CHUNK
        g = jnp.dot(hb_ref[...], wgu_ref[:, lo:lo + FF_CHUNK], preferred_element_type=F32)
        u = jnp.dot(hb_ref[...], wgu_ref[:, D_FF + lo:D_FF + lo + FF_CHUNK], preferred_element_type=F32)
        a = ((g * jax.nn.sigmoid(g)) * u).astype(BF16)
        acc_ref[...] += jnp.dot(a, wd_ref[lo:lo + FF_CHUNK, :], preferred_element_type=F32)
    gate = mod_ref[3 * sub + 2:3 * sub + 3, :]
    y = xm_ref[...] + (FFN_RES * gate) * acc_ref[...]
    if final:
        y = (y * lax.rsqrt(jnp.mean(y * y, axis=-1, keepdims=True) + EPS)) * fg_ref[...]
    o_ref[...] = y


def _ffn(xs, mods_i, ng_i, wgu, wd, layer, which, sub, attn=None, lru=None, w_o=None, final_g=None):
    final = final_g is not None
    n_tiles = X_TILES if final else N_TILES
    w_map = lambda i: (layer, which, 0, 0)
    row_spec = lambda cols: pl.BlockSpec((TM, cols), lambda i: (i, 0))
    in_specs = [
        row_spec(D_MODEL),
        pl.BlockSpec((None, MOD_ROWS, D_MODEL), lambda i: (_tile_kind(i), 0, 0)),
        _const_spec((SUBLANES, D_MODEL)),
        pl.BlockSpec((None, None, D_MODEL, 2 * D_FF), w_map, pipeline_mode=pl.Buffered(1)),
        pl.BlockSpec((None, None, D_FF, D_MODEL), w_map, pipeline_mode=pl.Buffered(1)),
    ]
    args = [xs, mods_i, ng_i, wgu, wd]
    mixer = None
    if attn is not None:
        mixer = "attn"
        in_specs += [row_spec(D_MODEL), _const_spec((D_MODEL, D_MODEL))]
        args += [attn, w_o]
    elif lru is not None:
        mixer = "lru"
        h_spec = pl.BlockSpec((LRU_BLOCKS, TM, LRU_BW), lambda i: (0, i, 0))
        in_specs += [h_spec, h_spec, row_spec(D_RNN), _const_spec((D_RNN, D_MODEL))]
        args += [*lru, w_o]
    if final:
        in_specs.append(_const_spec((1, D_MODEL)))
        args.append(final_g.reshape(1, D_MODEL))
    scratch = [pltpu.VMEM((TM, D_MODEL), BF16), pltpu.VMEM((TM, D_MODEL), F32), pltpu.VMEM((TM, D_MODEL), F32)]
    return pl.pallas_call(
        functools.partial(_ffn_kernel, sub=sub, mixer=mixer, final=final),
        grid=(n_tiles,),
        in_specs=in_specs,
        out_specs=row_spec(D_MODEL),
        out_shape=jax.ShapeDtypeStruct((n_tiles * TM, D_MODEL), F32),
        scratch_shapes=scratch,
        compiler_params=_cparams(("arbitrary",)),
        name="ffn_" + (mixer or "plain") + ("_final" if final else ""),
    )(*args)


NMM_COLS = 512


def _rope_tile(y, c, s1, s2):
    return y * c + pltpu.roll(y, LANES - 16, 1) * s1 + pltpu.roll(y, 16, 1) * s2


def _nmm_kernel(*refs, n_cols, rope_cols, out_dtype):
    if rope_cols:
        x_ref, mod_ref, ng_ref, w_ref, c_ref, s1_ref, s2_ref, o_ref = refs
    else:
        x_ref, mod_ref, ng_ref, w_ref, o_ref = refs
    hb = _adaln(x_ref[...], mod_ref, ng_ref, 1).astype(BF16)
    for c0 in range(0, n_cols, NMM_COLS):
        y = jnp.dot(hb, w_ref[:, c0:c0 + NMM_COLS], preferred_element_type=F32)
        if c0 < rope_cols:
            c, s1, s2 = c_ref[...], s1_ref[...], s2_ref[...]
            y = jnp.concatenate(
                [_rope_tile(y[:, t:t + LANES], c, s1, s2) for t in range(0, NMM_COLS, LANES)], axis=1)
        o_ref[:, c0:c0 + NMM_COLS] = y.astype(out_dtype)


def _nmm(xs, mods_i, ng_i, w, out_dtype, rope=None, rope_cols=0):
    n_cols = w.shape[1]
    in_specs = [
        pl.BlockSpec((TM, D_MODEL), lambda i: (i, 0)),
        pl.BlockSpec((None, MOD_ROWS, D_MODEL), lambda i: (_tile_kind(i), 0, 0)),
        _const_spec((SUBLANES, D_MODEL)),
        _const_spec((D_MODEL, n_cols)),
    ]
    args = [xs, mods_i, ng_i, w]
    if rope_cols:
        rmap = lambda i: (jnp.where(i < X_TILES, i % TILES_PER_BATCH, TILES_PER_BATCH), 0)
        in_specs += [pl.BlockSpec((TM, LANES), rmap)] * 3
        args += list(rope)
    return pl.pallas_call(
        functools.partial(_nmm_kernel, n_cols=n_cols, rope_cols=rope_cols, out_dtype=out_dtype),
        grid=(N_TILES,),
        in_specs=in_specs,
        out_specs=pl.BlockSpec((TM, n_cols), lambda i: (i, 0)),
        out_shape=jax.ShapeDtypeStruct((N_ALL, n_cols), out_dtype),
        compiler_params=_cparams(("arbitrary",)),
        name="adaln_proj",
    )(*args)


def _rope_tables():
    quarter = HEAD_DIM // 4
    pos = jnp.arange(SEQ)
    inv = ROPE_BASE ** (-jnp.arange(quarter, dtype=F32) / quarter)
    lane = np.arange(LANES)
    d = lane % HEAD_DIM
    use_col = (d >= HEAD_DIM // 2)
    first = (d % (HEAD_DIM // 2)) < quarter
    ang_row = (pos // GRID_W).astype(F32)[:, None] * inv
    ang_col = (pos % GRID_W).astype(F32)[:, None] * inv
    fi = d % quarter
    ang = jnp.where(use_col[None, :], ang_col[:, fi], ang_row[:, fi])
    cos, sin = jnp.cos(ang), jnp.sin(ang)
    s1 = jnp.where(first[None, :], -sin, 0.0)
    s2 = jnp.where(first[None, :], 0.0, sin)
    ident = jnp.ones((TM, LANES), F32)
    zero = jnp.zeros((TM, LANES), F32)
    return (jnp.concatenate([cos, ident]), jnp.concatenate([s1, zero]), jnp.concatenate([s2, zero]))


def _lru_in_kernel(x_ref, mod_ref, ng_ref, w_ref, g_ref, xr_ref):
    hb = _adaln(x_ref[...], mod_ref, ng_ref, 1).astype(BF16)
    for c0 in range(0, D_RNN, NMM_COLS):
        g_ref[:, c0:c0 + NMM_COLS] = jnp.dot(hb, w_ref[:, c0:c0 + NMM_COLS], preferred_element_type=F32)
    for c0 in range(0, D_RNN, NMM_COLS):
        y = jnp.dot(hb, w_ref[:, D_RNN + c0:D_RNN + c0 + NMM_COLS], preferred_element_type=F32)
        for t in range(NMM_COLS // LRU_BW):
            xr_ref[c0 // LRU_BW + t] = y[:, t * LRU_BW:(t + 1) * LRU_BW]


def _lru_in_proj(xs, mods_i, ng_i, w):
    return pl.pallas_call(
        _lru_in_kernel,
        grid=(N_TILES,),
        in_specs=[
            pl.BlockSpec((TM, D_MODEL), lambda i: (i, 0)),
            pl.BlockSpec((None, MOD_ROWS, D_MODEL), lambda i: (_tile_kind(i), 0, 0)),
            _const_spec((SUBLANES, D_MODEL)),
            _const_spec((D_MODEL, 2 * D_RNN)),
        ],
        out_specs=[pl.BlockSpec((TM, D_RNN), lambda i: (i, 0)),
                   pl.BlockSpec((LRU_BLOCKS, TM, LRU_BW), lambda i: (0, i, 0))],
        out_shape=[jax.ShapeDtypeStruct((N_ALL, D_RNN), F32),
                   jax.ShapeDtypeStruct((LRU_BLOCKS, N_ALL, LRU_BW), F32)],
        compiler_params=_cparams(("arbitrary",)),
        name="lru_in_proj",
    )(xs, mods_i, ng_i, w)


LRU_T = 256
LRU_SEG = LRU_T // SUBLANES
X_CHUNKS = SEQ // LRU_T
HALO = SUBLANES


def _softplus(x):
    return jnp.maximum(x, 0.0) + jnp.log1p(jnp.exp(-jnp.abs(x)))


def _sigmoid(x):
    return 0.5 * jnp.tanh(0.5 * x) + 0.5


def _lru_kernel(cur_ref, prev_ref, next_ref, cw_ref, gw_ref, gb_ref, o_ref,
                h_ref, p_ref, carry_ref, *, reverse):
    m = pl.program_id(1)
    is_x = m > 0
    c = (X_CHUNKS - m) if reverse else (m - 1)

    @pl.when(m == 0)
    def _():
        carry_ref[...] = jnp.zeros_like(carry_ref)

    has_prev = jnp.logical_and(is_x, c > 0)
    has_next = jnp.logical_and(is_x, c < X_CHUNKS - 1)
    sub = lax.broadcasted_iota(jnp.int32, (SUBLANES, LRU_BW), 0)
    steps = range(LRU_SEG - 1, -1, -1) if reverse else range(LRU_SEG)
    segs = range(SUBLANES - 1, -1, -1) if reverse else range(SUBLANES)

    for n in range(LRU_BLOCKS):
        cols = slice(n * LRU_BW, (n + 1) * LRU_BW)
        xg = [cur_ref[n, pl.ds(k, SUBLANES, stride=LRU_SEG), :] for k in range(LRU_SEG)]
        before = jnp.where(has_prev, prev_ref[n, HALO - 1:HALO, :], 0.0)
        after0 = jnp.where(has_next, next_ref[n, 0:1, :], 0.0)
        after1 = jnp.where(has_next, next_ref[n, 1:2, :], 0.0)
        xm1 = jnp.where(sub == 0, before, pltpu.roll(xg[LRU_SEG - 1], 1, 0))
        xp0 = jnp.where(sub == SUBLANES - 1, after0, pltpu.roll(xg[0], SUBLANES - 1, 0))
        xp1 = jnp.where(sub == SUBLANES - 1, after1, pltpu.roll(xg[1], SUBLANES - 1, 0))
        xe = [xm1] + xg + [xp0, xp1]
        cb = cw_ref[CONV_W:CONV_W + 1, cols]
        taps = [cw_ref[j:j + 1, cols] for j in range(CONV_W)]
        xc = jnp.concatenate(
            [cb + xe[k] * taps[0] + xe[k + 1] * taps[1] + xe[k + 2] * taps[2] + xe[k + 3] * taps[3]
             for k in range(LRU_SEG)], axis=0)
        ri = jnp.dot(xc.astype(BF16), gw_ref[n], preferred_element_type=F32)
        r = _sigmoid(ri[:, :LRU_BW] + gb_ref[0:1, cols])
        i = _sigmoid(ri[:, LRU_BW:] + gb_ref[1:2, cols])
        log_a = (-LRU_C * r) * _softplus(-gb_ref[2:3, cols])
        a = jnp.exp(log_a)
        t = jnp.tanh(log_a)
        z = (-2.0 * t) / (1.0 - t)
        b = jnp.where(z > 0.0, z * lax.rsqrt(z), 0.0) * (i * xc)

        h = jnp.zeros((SUBLANES, LRU_BW), F32)
        p = jnp.ones((SUBLANES, LRU_BW), F32)
        for k in steps:
            grp = slice(k * SUBLANES, (k + 1) * SUBLANES)
            h = a[grp] * h + b[grp]
            p = a[grp] * p
            h_ref[n, grp, :] = h
            p_ref[n, grp, :] = p
        st = carry_ref[n, 0:1, :]
        enter = jnp.zeros((SUBLANES, LRU_BW), F32)
        for s in segs:
            enter = jnp.where(sub == s, st, enter)
            st = p[s:s + 1, :] * st + h[s:s + 1, :]
        carry_ref[n, 0:1, :] = st
        for k in range(LRU_SEG):
            grp = slice(k * SUBLANES, (k + 1) * SUBLANES)
            o_ref[n, pl.ds(k, SUBLANES, stride=LRU_SEG), :] = h_ref[n, grp, :] + p_ref[n, grp, :] * enter


def _lru_scan(xr, cw, gw, gb, reverse):
    blocks_per_chunk = LRU_T // HALO
    n_halo_blocks = N_ALL // HALO

    def row_block(b, m):
        c = (X_CHUNKS - m) if reverse else (m - 1)
        return jnp.where(m == 0, N_X // LRU_T + b, b * X_CHUNKS + c)

    cur = lambda b, m: (0, row_block(b, m), 0)
    prev = lambda b, m: (0, jnp.maximum(row_block(b, m) * blocks_per_chunk - 1, 0), 0)
    nxt = lambda b, m: (0, jnp.minimum((row_block(b, m) + 1) * blocks_per_chunk, n_halo_blocks - 1), 0)
    return pl.pallas_call(
        functools.partial(_lru_kernel, reverse=reverse),
        grid=(BATCH, X_CHUNKS + 1),
        in_specs=[
            pl.BlockSpec((LRU_BLOCKS, LRU_T, LRU_BW), cur),
            pl.BlockSpec((LRU_BLOCKS, HALO, LRU_BW), prev),
            pl.BlockSpec((LRU_BLOCKS, HALO, LRU_BW), nxt),
            _const_spec((SUBLANES, D_RNN)),
            _const_spec((LRU_BLOCKS, LRU_BW, 2 * LRU_BW)),
            _const_spec((SUBLANES, D_RNN)),
        ],
        out_specs=pl.BlockSpec((LRU_BLOCKS, LRU_T, LRU_BW), cur),
        out_shape=jax.ShapeDtypeStruct((LRU_BLOCKS, N_ALL, LRU_BW), F32),
        scratch_shapes=[
            pltpu.VMEM((LRU_BLOCKS, LRU_T, LRU_BW), F32),
            pltpu.VMEM((LRU_BLOCKS, LRU_T, LRU_BW), F32),
            pltpu.VMEM((LRU_BLOCKS, SUBLANES, LRU_BW), F32),
        ],
        compiler_params=_cparams(("arbitrary", "arbitrary")),
        name="lru_bwd" if reverse else "lru_fwd",
    )(xr, xr, xr, cw, gw, gb)


SWA_TQ = 128
SWA_XQ = SEQ // SWA_TQ
SWA_CQ = CTX_LEN // SWA_TQ
SWA_SPAN = 3 * SWA_TQ
KV_COLS = 2 * B_KV_HEADS * LANES
_NT = (((1,), (1,)), ((), ()))


def _split_heads(qt):
    lo = lax.broadcasted_iota(jnp.int32, qt.shape, 1) < HEAD_DIM
    zero = jnp.zeros_like(qt)
    return jnp.concatenate([jnp.where(lo, qt, zero), jnp.where(lo, zero, qt)], axis=0)


def _merge_heads(o):
    t = o.shape[0] // 2
    lo = lax.broadcasted_iota(jnp.int32, (t, LANES), 1) < HEAD_DIM
    return jnp.where(lo, o[:t], o[t:])


def _swa_kernel(sink_ref, q_ref, kvp_ref, kvc_ref, kvn_ref, kvx_ref, o_ref):
    j = pl.program_id(1)
    iq = lax.broadcasted_iota(jnp.int32, (SWA_TQ, SWA_SPAN), 0)
    ik = lax.broadcasted_iota(jnp.int32, (SWA_TQ, SWA_SPAN), 1)
    lo_ok = jnp.where(j > 0, 0, SWA_TQ)
    hi_ok = jnp.where(j < SWA_XQ - 1, SWA_SPAN, 2 * SWA_TQ)
    hi_ok = jnp.where(j < SWA_XQ, hi_ok, 0)
    dist = iq + WINDOW - ik
    bias1 = jnp.where(jnp.abs(dist) <= WINDOW, 0.0, NEG)
    bias1 = jnp.where(ik >= lo_ok, bias1, NEG)
    bias1 = jnp.where(ik < hi_ok, bias1, NEG)
    bias = jnp.concatenate([bias1, bias1], axis=0)
    top = lax.broadcasted_iota(jnp.int32, (2 * SWA_TQ, 1), 0) < SWA_TQ
    for g in range(B_KV_HEADS):
        kcol = slice(g * LANES, (g + 1) * LANES)
        vcol = slice((B_KV_HEADS + g) * LANES, (B_KV_HEADS + g + 1) * LANES)
        kd = jnp.concatenate([kvp_ref[:, kcol], kvc_ref[:, kcol], kvn_ref[:, kcol]], axis=0)
        vd = jnp.concatenate([kvp_ref[:, vcol], kvc_ref[:, vcol], kvn_ref[:, vcol]], axis=0)
        kx = kvx_ref[:, kcol]
        vx = kvx_ref[:, vcol]
        for t in range(2):
            tile = 2 * g + t
            qs = _split_heads(q_ref[:, tile * LANES:(tile + 1) * LANES])
            s_loc = lax.dot_general(qs, kd, _NT, preferred_element_type=F32) + bias
            s_ctx = lax.dot_general(qs, kx, _NT, preferred_element_type=F32)
            sink = jnp.where(top, sink_ref[2 * tile], sink_ref[2 * tile + 1])
            mx = jnp.maximum(jnp.maximum(jnp.max(s_loc, axis=-1, keepdims=True),
                                         jnp.max(s_ctx, axis=-1, keepdims=True)), sink)
            p_loc = jnp.exp(s_loc - mx)
            p_ctx = jnp.exp(s_ctx - mx)
            den = (jnp.sum(p_loc, axis=-1, keepdims=True) + jnp.sum(p_ctx, axis=-1, keepdims=True)
                   + jnp.exp(sink - mx))
            o = (jnp.dot(p_loc.astype(BF16), vd, preferred_element_type=F32)
                 + jnp.dot(p_ctx.astype(BF16), vx, preferred_element_type=F32)) / den
            o_ref[:, tile * LANES:(tile + 1) * LANES] = _merge_heads(o).astype(BF16)


def _swa_attention(qkv, sinks):
    nq = SWA_XQ + SWA_CQ

    def q_block(b, j):
        return jnp.where(j < SWA_XQ, b * SWA_XQ + j, N_X // SWA_TQ + b * SWA_CQ + (j - SWA_XQ))

    def kv_block(off):
        return lambda b, j: (b * SWA_XQ + jnp.clip(j + off, 0, SWA_XQ - 1), 1)

    kv_spec = lambda off: pl.BlockSpec((SWA_TQ, KV_COLS), kv_block(off))
    return pl.pallas_call(
        _swa_kernel,
        grid=(BATCH, nq),
        in_specs=[
            pl.BlockSpec(memory_space=pltpu.SMEM),
            pl.BlockSpec((SWA_TQ, D_MODEL), lambda b, j: (q_block(b, j), 0)),
            kv_spec(-1), kv_spec(0), kv_spec(1),
            pl.BlockSpec((CTX_LEN, KV_COLS), lambda b, j: (N_X // CTX_LEN + b, 1)),
        ],
        out_specs=pl.BlockSpec((SWA_TQ, D_MODEL), lambda b, j: (q_block(b, j), 0)),
        out_shape=jax.ShapeDtypeStruct((N_ALL, D_MODEL), BF16),
        compiler_params=_cparams(("arbitrary", "arbitrary")),
        name="swa_attn",
    )(sinks, qkv, qkv, qkv, qkv, qkv)


def _prep_swa_weights(w_qkv):
    nq = B_HEADS * HEAD_DIM
    nk = B_KV_HEADS * HEAD_DIM
    wq = w_qkv[:, :nq] * (HEAD_DIM ** -0.5)
    dup = lambda w: jnp.tile(w.reshape(D_MODEL, B_KV_HEADS, 1, HEAD_DIM), (1, 1, 2, 1)).reshape(D_MODEL, -1)
    wk = dup(w_qkv[:, nq:nq + nk])
    wv = dup(w_qkv[:, nq + nk:])
    return jnp.concatenate([wq, wk, wv], axis=1).astype(BF16)


NAT_ROWS = 4
NAT_TQ = NAT_ROWS * GRID_W
NAT_XQ = SEQ // NAT_TQ
NAT_SPAN = 3 * NAT_TQ
N_HEAD_TILES = C_HEADS * HEAD_DIM // LANES
GRID_ROWS = SEQ // GRID_W


NAT_KEY_ROWS = NAT_SPAN // GRID_W
NAT_RPB_ROWS = 2 * WIN_H
NAT_PATTERNS = 4


def _nat_pattern(j):
    return jnp.where(j == 0, 0, jnp.where(j == NAT_XQ - 1, 2, jnp.where(j == NAT_XQ, 3, 1)))


def _nat_kernel(sel_ref, q_ref, kp_ref, kc_ref, kn_ref, vp_ref, vc_ref, vn_ref, kx_ref, vx_ref, tab_ref,
                o_ref, bias_ref):
    j = pl.program_id(1)

    @pl.when(jnp.logical_or(j <= 1, j >= NAT_XQ - 1))
    def _():
        lo = lax.broadcasted_iota(jnp.int32, (GRID_W, LANES), 1) < GRID_W
        base = _nat_pattern(j) * (NAT_ROWS * NAT_KEY_ROWS)
        for qr in range(NAT_ROWS):
            rows = slice(qr * GRID_W, (qr + 1) * GRID_W)
            for kp in range(NAT_KEY_ROWS // 2):
                e0 = sel_ref[base + qr * NAT_KEY_ROWS + 2 * kp]
                e1 = sel_ref[base + qr * NAT_KEY_ROWS + 2 * kp + 1]

                def fill(h, carry, rows=rows, kp=kp, e0=e0, e1=e1):
                    bias_ref[h, rows, kp * LANES:(kp + 1) * LANES] = jnp.where(lo, tab_ref[h, e0], tab_ref[h, e1])
                    return carry

                lax.fori_loop(0, C_HEADS, fill, 0)

    for t in range(N_HEAD_TILES):
        cols = slice(t * LANES, (t + 1) * LANES)
        qs = _split_heads(q_ref[:, cols])
        kd = jnp.concatenate([kp_ref[:, cols], kc_ref[:, cols], kn_ref[:, cols]], axis=0)
        vd = jnp.concatenate([vp_ref[:, cols], vc_ref[:, cols], vn_ref[:, cols]], axis=0)
        bias = bias_ref[2 * t:2 * t + 2].reshape(2 * NAT_TQ, NAT_SPAN)
        s_loc = lax.dot_general(qs, kd, _NT, preferred_element_type=F32) + bias
        s_ctx = lax.dot_general(qs, kx_ref[:, cols], _NT, preferred_element_type=F32)
        mx = jnp.maximum(jnp.max(s_loc, axis=-1, keepdims=True), jnp.max(s_ctx, axis=-1, keepdims=True))
        p_loc = jnp.exp(s_loc - mx)
        p_ctx = jnp.exp(s_ctx - mx)
        den = jnp.sum(p_loc, axis=-1, keepdims=True) + jnp.sum(p_ctx, axis=-1, keepdims=True)
        o = (jnp.dot(p_loc.astype(BF16), vd, preferred_element_type=F32)
             + jnp.dot(p_ctx.astype(BF16), vx_ref[:, cols], preferred_element_type=F32)) / den
        o_ref[:, cols] = _merge_heads(o).astype(BF16)


def _nat_attention(qkv, row_sel, tab):
    nq = NAT_XQ + 1

    def q_block(b, j):
        return jnp.where(j < NAT_XQ, b * NAT_XQ + j, N_X // NAT_TQ + b)

    def kv_spec(off, part):
        return pl.BlockSpec(
            (NAT_TQ, D_MODEL), lambda b, j: (b * NAT_XQ + jnp.clip(j + off, 0, NAT_XQ - 1), part))

    def ctx_spec(part):
        return pl.BlockSpec((CTX_LEN, D_MODEL), lambda b, j: (N_X // CTX_LEN + b, part))

    return pl.pallas_call(
        _nat_kernel,
        grid=(BATCH, nq),
        in_specs=[
            pl.BlockSpec(memory_space=pltpu.SMEM),
            pl.BlockSpec((NAT_TQ, D_MODEL), lambda b, j: (q_block(b, j), 0)),
            kv_spec(-1, 1), kv_spec(0, 1), kv_spec(1, 1),
            kv_spec(-1, 2), kv_spec(0, 2), kv_spec(1, 2),
            ctx_spec(1), ctx_spec(2),
            _const_spec((C_HEADS, NAT_RPB_ROWS, GRID_W, LANES)),
        ],
        out_specs=pl.BlockSpec((NAT_TQ, D_MODEL), lambda b, j: (q_block(b, j), 0)),
        out_shape=jax.ShapeDtypeStruct((N_ALL, D_MODEL), BF16),
        scratch_shapes=[pltpu.VMEM((C_HEADS, NAT_TQ, NAT_SPAN), F32)],
        compiler_params=_cparams(("arbitrary", "arbitrary")),
        name="nat_attn",
    )(row_sel, qkv, qkv, qkv, qkv, qkv, qkv, qkv, qkv, qkv, tab)


def _nat_row_select():
    qr = np.arange(NAT_ROWS)
    kr = np.arange(NAT_KEY_ROWS) - NAT_ROWS
    row_sel = []
    for r0 in (0, NAT_ROWS, GRID_ROWS - NAT_ROWS):
        r = r0 + qr
        rs = np.clip(r - WIN_H // 2, 0, GRID_ROWS - WIN_H)
        k_abs = r0 + kr
        row_ok = (k_abs[None, :] >= rs[:, None]) & (k_abs[None, :] < rs[:, None] + WIN_H)
        row_sel.append(np.where(row_ok, k_abs[None, :] - r[:, None] + (WIN_H - 1), NAT_RPB_ROWS - 1))
    row_sel.append(np.full((NAT_ROWS, NAT_KEY_ROWS), NAT_RPB_ROWS - 1))
    return jnp.asarray(np.stack(row_sel).reshape(-1), jnp.int32)


def _nat_bias_tiles(rpb):
    col = np.arange(GRID_W)
    col_start = np.clip(col - WIN_W // 2, 0, GRID_W - WIN_W)
    col_ok = (col[None, :] >= col_start[:, None]) & (col[None, :] < col_start[:, None] + WIN_W)
    col_off = np.clip(col[None, :] - col[:, None] + (WIN_W - 1), 0, 2 * WIN_W - 2)
    by_col = jnp.where(col_ok[None, None], rpb[:, :, col_off], NEG)
    by_col = jnp.concatenate([by_col, jnp.full((C_HEADS, 1, GRID_W, GRID_W), NEG, F32)], axis=1)
    return jnp.concatenate([by_col, by_col], axis=-1)


def _prep_nat_weights(w_qkv):
    nq = C_HEADS * HEAD_DIM
    return jnp.concatenate([w_qkv[:, :nq] * (HEAD_DIM ** -0.5), w_qkv[:, nq:]], axis=1).astype(BF16)


def _pad_rows(a, rows=SUBLANES):
    return jnp.pad(a, ((0, rows - a.shape[0]), (0, 0)))


def kernel(x, c, ctx, c_ctx, w_ada, b_ada, norm_g, w_ffn_gu, w_ffn_down, a_w_in, a_conv_w, a_conv_b,
           a_gate_w, a_gate_b, a_lambda, a_w_out, b_w_qkv, b_sinks, b_w_o, c_w_qkv, c_rpb, c_w_o, final_g):
    xs = jnp.concatenate([x.reshape(N_X, D_MODEL), ctx.reshape(N_CTX, D_MODEL)], axis=0)
    cvec = _pad_rows(jnp.concatenate([c, c_ctx[None, :]], axis=0))
    mods = _ada_mods(cvec, w_ada, b_ada)
    rope = _rope_tables()
    wgu, wd = w_ffn_gu.astype(BF16), w_ffn_down.astype(BF16)
    for i in range(DEPTH):
        kind, j = i % N_MIXERS, i // N_MIXERS
        last = i == DEPTH - 1
        ng = _pad_rows(norm_g[i])
        xs = _ffn(xs, mods[i], ng, wgu, wd, i, 0, 0)
        if kind == 0:
            g_branch, xr = _lru_in_proj(xs, mods[i], ng, a_w_in[j].astype(BF16))
            cw = _pad_rows(jnp.concatenate([a_conv_w[j], a_conv_b[j][None, :]], axis=0))
            hs = []
            for d in range(2):
                gb = _pad_rows(jnp.concatenate([a_gate_b[j, d], a_lambda[j, d][None, :]], axis=0))
                gw = jnp.concatenate([a_gate_w[j, d, 0], a_gate_w[j, d, 1]], axis=-1).astype(BF16)
                hs.append(_lru_scan(xr, cw, gw, gb, reverse=bool(d)))
            mixed = dict(lru=(hs[0], hs[1], g_branch), w_o=a_w_out[j].astype(BF16))
        elif kind == 1:
            qkv = _nmm(xs, mods[i], ng, _prep_swa_weights(b_w_qkv[j]), BF16, rope=rope,
                       rope_cols=B_HEADS * HEAD_DIM + B_KV_HEADS * LANES)
            mixed = dict(attn=_swa_attention(qkv, b_sinks[j]), w_o=b_w_o[j].astype(BF16))
        else:
            qkv = _nmm(xs, mods[i], ng, _prep_nat_weights(c_w_qkv[j]), BF16)
            o = _nat_attention(qkv, _nat_row_select(), _nat_bias_tiles(c_rpb[j]))
            mixed = dict(attn=o, w_o=c_w_o[j].astype(BF16))
        xs = _ffn(xs, mods[i], ng, wgu, wd, i, 1, 2, final_g=final_g if last else None, **mixed)
    return xs.reshape(BATCH, SEQ, D_MODEL)
```

```python
import functools

import numpy as np
import jax
import jax.numpy as jnp
from jax import lax
from jax.experimental import pallas as pl
from jax.experimental.pallas import tpu as pltpu

D_MODEL = 1024
BATCH = 2
SEQ = 16384
DEPTH = 4
GRID_W = 64
CTX_LEN = 256
N_MIXERS = 3
EPS = 1e-6
FFN_RES = 0.5
D_FF = 2816
D_RNN = 1536
LRU_BLOCKS = 12
LRU_BW = D_RNN // LRU_BLOCKS
CONV_W = 4
LRU_C = 8.0
HEAD_DIM = 64
B_HEADS = 16
B_KV_HEADS = 4
WINDOW = 128
ROPE_BASE = 10000.0
C_HEADS = 16
WIN_H = 8
WIN_W = 16
NEG = -1e30

LANES = 128
SUBLANES = 8
VMEM_LIMIT = 56 * 1024 * 1024

N_X = BATCH * SEQ
N_CTX = BATCH * CTX_LEN
N_ALL = N_X + N_CTX
TM = 512
N_TILES = N_ALL // TM
X_TILES = N_X // TM
TILES_PER_BATCH = SEQ // TM
MOD_ROWS = 16
FF_CHUNK = 256
N_FF_CHUNKS = D_FF // FF_CHUNK

F32 = jnp.float32
BF16 = jnp.bfloat16


def _cparams(sem):
    return pltpu.CompilerParams(dimension_semantics=sem, vmem_limit_bytes=VMEM_LIMIT)


def _const_spec(shape):
    nd = len(shape)
    return pl.BlockSpec(shape, lambda *_: (0,) * nd, pipeline_mode=pl.Buffered(1))


def _tile_kind(i):
    return jnp.minimum(i // TILES_PER_BATCH, BATCH)


def _adaln(x, mod_ref, ng_ref, sub):
    g = ng_ref[sub:sub + 1, :]
    shift = mod_ref[3 * sub:3 * sub + 1, :]
    scale = mod_ref[3 * sub + 1:3 * sub + 2, :]
    y = x * lax.rsqrt(jnp.mean(x * x, axis=-1, keepdims=True) + EPS)
    return (y * g) * (1.0 + scale) + shift


ADA_COLS = 1152


def _ada_kernel(c_ref, w_ref, b_ref, o_ref):
    c = c_ref[...]
    sc = c * jax.nn.sigmoid(c)
    o_ref[...] = jnp.dot(sc, w_ref[...], preferred_element_type=F32,
                         precision=lax.Precision.HIGHEST) + b_ref[...]


def _ada_mods(cvec, w_ada, b_ada):
    out = pl.pallas_call(
        _ada_kernel,
        grid=(DEPTH, 9 * D_MODEL // ADA_COLS),
        in_specs=[
            pl.BlockSpec((SUBLANES, D_MODEL), lambda i, j: (0, 0)),
            pl.BlockSpec((None, D_MODEL, ADA_COLS), lambda i, j: (i, 0, j)),
            pl.BlockSpec((None, 1, ADA_COLS), lambda i, j: (i, 0, j)),
        ],
        out_specs=pl.BlockSpec((None, SUBLANES, ADA_COLS), lambda i, j: (i, 0, j)),
        out_shape=jax.ShapeDtypeStruct((DEPTH, SUBLANES, 9 * D_MODEL), F32),
        compiler_params=_cparams(("arbitrary", "arbitrary")),
        name="ada_mods",
    )(cvec, w_ada, b_ada.reshape(DEPTH, 1, 9 * D_MODEL))
    mods = out.reshape(DEPTH, SUBLANES, 9, D_MODEL)[:, :BATCH + 1]
    return jnp.pad(mods, ((0, 0), (0, 0), (0, MOD_ROWS - 9), (0, 0)))


def _gelu_tanh(g):
    return 0.5 * g * (1.0 + jnp.tanh(np.sqrt(2.0 / np.pi).astype(np.float32) * (g + 0.044715 * (g * g * g))))


def _ffn_kernel(*refs, sub, mixer, final):
    x_ref, mod_ref, ng_ref, wgu_ref, wd_ref = refs[:5]
    rest = list(refs[5:])
    if mixer == "attn":
        a_ref, wo_ref = rest[:2]
        rest = rest[2:]
        mixed = a_ref[...]
    elif mixer == "lru":
        hf_ref, hbk_ref, g_ref, wo_ref = rest[:4]
        rest = rest[4:]
        h = jnp.concatenate(
            [jnp.concatenate([_from_scan_order(hf_ref[n, ch * LRU_T:(ch + 1) * LRU_T, :]
                                               + hbk_ref[n, ch * LRU_T:(ch + 1) * LRU_T, :])
                              for ch in range(CHUNKS_PER_TILE)], axis=0)
             for n in range(LRU_BLOCKS)], axis=1)
        mixed = (h * _gelu_tanh(g_ref[...])).astype(BF16)
    elif mixer == "join":
        ctx_ref = rest.pop(0)
    fg_ref = rest.pop(0) if final else None
    o_ref, hb_ref, acc_ref, xm_ref = rest
    if mixer is None:
        xm_ref = x_ref
    elif mixer == "join":
        xm_ref[...] = jnp.where(pl.program_id(0) < X_TILES, x_ref[...], ctx_ref[...])
    else:
        y = jnp.dot(mixed, wo_ref[...], preferred_element_type=F32)
        xm_ref[...] = x_ref[...] + mod_ref[5:6, :] * y
    hb_ref[...] = _adaln(xm_ref[...], mod_ref, ng_ref, sub).astype(BF16)
    acc_ref[...] = jnp.zeros_like(acc_ref)

    for c in range(N_FF_CHUNKS):
        lo = c * FF_CHUNK
        g = jnp.dot(hb_ref[...], wgu_ref[:, lo:lo + FF_CHUNK], preferred_element_type=F32)
        u = jnp.dot(hb_ref[...], wgu_ref[:, D_FF + lo:D_FF + lo + FF_CHUNK], preferred_element_type=F32)
        a = ((g * jax.nn.sigmoid(g)) * u).astype(BF16)
        acc_ref[...] += jnp.dot(a, wd_ref[lo:lo + FF_CHUNK, :], preferred_element_type=F32)
    gate = mod_ref[3 * sub + 2:3 * sub + 3, :]
    y = xm_ref[...] + (FFN_RES * gate) * acc_ref[...]
    if final:
        y = (y * lax.rsqrt(jnp.mean(y * y, axis=-1, keepdims=True) + EPS)) * fg_ref[...]
    o_ref[...] = y


def _ffn(xs, mods_i, ng_i, wgu, wd, layer, which, sub, ctx=None, attn=None, lru=None, w_o=None, final_g=None):
    final = final_g is not None
    n_tiles = X_TILES if final else N_TILES
    w_map = lambda i: (layer, which, 0, 0)
    row_spec = lambda cols: pl.BlockSpec((TM, cols), lambda i: (i, 0))
    x_spec = row_spec(D_MODEL)
    if ctx is not None:
        x_spec = pl.BlockSpec((TM, D_MODEL), lambda i: (jnp.minimum(i, X_TILES - 1), 0))
    in_specs = [
        x_spec,
        pl.BlockSpec((None, MOD_ROWS, D_MODEL), lambda i: (_tile_kind(i), 0, 0)),
        _const_spec((SUBLANES, D_MODEL)),
        pl.BlockSpec((None, None, D_MODEL, 2 * D_FF), w_map, pipeline_mode=pl.Buffered(1)),
        pl.BlockSpec((None, None, D_FF, D_MODEL), w_map, pipeline_mode=pl.Buffered(1)),
    ]
    args = [xs, mods_i, ng_i, wgu, wd]
    mixer = None
    if attn is not None:
        mixer = "attn"
        in_specs += [row_spec(D_MODEL), _const_spec((D_MODEL, D_MODEL))]
        args += [attn, w_o]
    elif lru is not None:
        mixer = "lru"
        h_spec = pl.BlockSpec((LRU_BLOCKS, TM, LRU_BW), lambda i: (0, i, 0))
        in_specs += [h_spec, h_spec, row_spec(D_RNN), _const_spec((D_RNN, D_MODEL))]
        args += [*lru, w_o]
    elif ctx is not None:
        mixer = "join"
        in_specs.append(_const_spec((N_CTX, D_MODEL)))
        args.append(ctx)
    if final:
        in_specs.append(_const_spec((1, D_MODEL)))
        args.append(final_g.reshape(1, D_MODEL))
    scratch = [pltpu.VMEM((TM, D_MODEL), BF16), pltpu.VMEM((TM, D_MODEL), F32), pltpu.VMEM((TM, D_MODEL), F32)]
    return pl.pallas_call(
        functools.partial(_ffn_kernel, sub=sub, mixer=mixer, final=final),
        grid=(n_tiles,),
        in_specs=in_specs,
        out_specs=row_spec(D_MODEL),
        out_shape=jax.ShapeDtypeStruct((n_tiles * TM, D_MODEL), F32),
        scratch_shapes=scratch,
        compiler_params=_cparams(("arbitrary",)),
        name="ffn_" + (mixer or "plain") + ("_final" if final else ""),
    )(*args)


NMM_COLS = 512


def _rope_tile(y, c, s1, s2):
    return y * c + pltpu.roll(y, LANES - 16, 1) * s1 + pltpu.roll(y, 16, 1) * s2


def _nmm_kernel(*refs, n_cols, rope_cols, out_dtype):
    if rope_cols:
        x_ref, mod_ref, ng_ref, w_ref, c_ref, s1_ref, s2_ref, o_ref = refs
    else:
        x_ref, mod_ref, ng_ref, w_ref, o_ref = refs
    hb = _adaln(x_ref[...], mod_ref, ng_ref, 1).astype(BF16)
    for c0 in range(0, n_cols, NMM_COLS):
        y = jnp.dot(hb, w_ref[:, c0:c0 + NMM_COLS], preferred_element_type=F32)
        if c0 < rope_cols:
            c, s1, s2 = c_ref[...], s1_ref[...], s2_ref[...]
            y = jnp.concatenate(
                [_rope_tile(y[:, t:t + LANES], c, s1, s2) for t in range(0, NMM_COLS, LANES)], axis=1)
        o_ref[:, c0:c0 + NMM_COLS] = y.astype(out_dtype)


def _nmm(xs, mods_i, ng_i, w, out_dtype, rope=None, rope_cols=0):
    n_cols = w.shape[1]
    in_specs = [
        pl.BlockSpec((TM, D_MODEL), lambda i: (i, 0)),
        pl.BlockSpec((None, MOD_ROWS, D_MODEL), lambda i: (_tile_kind(i), 0, 0)),
        _const_spec((SUBLANES, D_MODEL)),
        _const_spec((D_MODEL, n_cols)),
    ]
    args = [xs, mods_i, ng_i, w]
    if rope_cols:
        rmap = lambda i: (jnp.where(i < X_TILES, i % TILES_PER_BATCH, TILES_PER_BATCH), 0)
        in_specs += [pl.BlockSpec((TM, LANES), rmap)] * 3
        args += list(rope)
    return pl.pallas_call(
        functools.partial(_nmm_kernel, n_cols=n_cols, rope_cols=rope_cols, out_dtype=out_dtype),
        grid=(N_TILES,),
        in_specs=in_specs,
        out_specs=pl.BlockSpec((TM, n_cols), lambda i: (i, 0)),
        out_shape=jax.ShapeDtypeStruct((N_ALL, n_cols), out_dtype),
        compiler_params=_cparams(("arbitrary",)),
        name="adaln_proj",
    )(*args)


def _rope_tables():
    quarter = HEAD_DIM // 4
    pos = jnp.arange(SEQ)
    inv = ROPE_BASE ** (-jnp.arange(quarter, dtype=F32) / quarter)
    lane = np.arange(LANES)
    d = lane % HEAD_DIM
    use_col = (d >= HEAD_DIM // 2)
    first = (d % (HEAD_DIM // 2)) < quarter
    ang_row = (pos // GRID_W).astype(F32)[:, None] * inv
    ang_col = (pos % GRID_W).astype(F32)[:, None] * inv
    fi = d % quarter
    ang = jnp.where(use_col[None, :], ang_col[:, fi], ang_row[:, fi])
    cos, sin = jnp.cos(ang), jnp.sin(ang)
    s1 = jnp.where(first[None, :], -sin, 0.0)
    s2 = jnp.where(first[None, :], 0.0, sin)
    ident = jnp.ones((TM, LANES), F32)
    zero = jnp.zeros((TM, LANES), F32)
    return (jnp.concatenate([cos, ident]), jnp.concatenate([s1, zero]), jnp.concatenate([s2, zero]))


LRU_T = 256
LRU_SEG = LRU_T // SUBLANES
X_CHUNKS = SEQ // LRU_T
HALO = SUBLANES
CHUNKS_PER_TILE = TM // LRU_T
PROJ_GROUP = 256


def _to_scan_order(v):
    return pltpu.einshape("skl->ksl", v.reshape(SUBLANES, LRU_SEG, LRU_BW)).reshape(LRU_T, LRU_BW)


def _from_scan_order(v):
    return pltpu.einshape("ksl->skl", v.reshape(LRU_SEG, SUBLANES, LRU_BW)).reshape(LRU_T, LRU_BW)


def _lru_in_kernel(x_ref, xp_ref, xn_ref, mod_ref, ng_ref, w_ref, cw_ref, g_ref, xc_ref, hb_ref):
    i = pl.program_id(0)
    is_ctx = i >= X_TILES
    first = i % TILES_PER_BATCH == 0
    last = i % TILES_PER_BATCH == TILES_PER_BATCH - 1
    hb_ref[...] = _adaln(x_ref[...], mod_ref, ng_ref, 1).astype(BF16)
    halo = jnp.concatenate([xp_ref[...], xn_ref[...]], axis=0)
    hh = _adaln(halo, mod_ref, ng_ref, 1).astype(BF16)
    sub = lax.broadcasted_iota(jnp.int32, (SUBLANES, LRU_BW), 0)
    pad_before = [jnp.logical_or(is_ctx, first) if ch == 0 else is_ctx for ch in range(CHUNKS_PER_TILE)]
    pad_after = [jnp.logical_or(is_ctx, last) if ch == CHUNKS_PER_TILE - 1 else is_ctx
                 for ch in range(CHUNKS_PER_TILE)]

    def conv(n, ch, chunk, before, after):
        cols = slice(n * LRU_BW, (n + 1) * LRU_BW)
        cb = cw_ref[CONV_W:CONV_W + 1, cols]
        taps = [cw_ref[j:j + 1, cols] for j in range(CONV_W)]
        before = jnp.where(pad_before[ch], 0.0, before)
        after = jnp.where(pad_after[ch], 0.0, after)
        xp = _to_scan_order(chunk)
        xg = [xp[k * SUBLANES:(k + 1) * SUBLANES] for k in range(LRU_SEG)]
        xm1 = jnp.where(sub == 0, before, pltpu.roll(xg[LRU_SEG - 1], 1, 0))
        xp0 = jnp.where(sub == SUBLANES - 1, after[0:1], pltpu.roll(xg[0], SUBLANES - 1, 0))
        xp1 = jnp.where(sub == SUBLANES - 1, after[1:2], pltpu.roll(xg[1], SUBLANES - 1, 0))
        xe = [xm1] + xg + [xp0, xp1]
        for k in range(LRU_SEG):
            xc = cb + xe[k] * taps[0] + xe[k + 1] * taps[1] + xe[k + 2] * taps[2] + xe[k + 3] * taps[3]
            xc_ref[n, ch * LRU_T + k * SUBLANES:ch * LRU_T + (k + 1) * SUBLANES, :] = xc

    for c0 in range(0, D_RNN, PROJ_GROUP):
        wx = w_ref[:, D_RNN + c0:D_RNN + c0 + PROJ_GROUP]
        y = jnp.dot(hb_ref[...], wx, preferred_element_type=F32)
        yh = jnp.dot(hh, wx, preferred_element_type=F32)
        g_ref[:, c0:c0 + PROJ_GROUP] = jnp.dot(hb_ref[...], w_ref[:, c0:c0 + PROJ_GROUP],
                                               preferred_element_type=F32)
        for t in range(PROJ_GROUP // LRU_BW):
            lanes = slice(t * LRU_BW, (t + 1) * LRU_BW)
            ext = jnp.concatenate([yh[0:HALO, lanes], y[:, lanes], yh[HALO:, lanes]], axis=0)
            for ch in range(CHUNKS_PER_TILE):
                lo = HALO + ch * LRU_T
                conv(c0 // LRU_BW + t, ch, ext[lo:lo + LRU_T], ext[lo - 1:lo], ext[lo + LRU_T:lo + LRU_T + 2])


def _lru_in_proj(xs, mods_i, ng_i, w, cw):
    blocks_per_tile = TM // HALO
    n_halo_blocks = N_ALL // HALO
    return pl.pallas_call(
        _lru_in_kernel,
        grid=(N_TILES,),
        in_specs=[
            pl.BlockSpec((TM, D_MODEL), lambda i: (i, 0)),
            pl.BlockSpec((HALO, D_MODEL), lambda i: (jnp.maximum(i * blocks_per_tile - 1, 0), 0)),
            pl.BlockSpec((HALO, D_MODEL), lambda i: (jnp.minimum((i + 1) * blocks_per_tile, n_halo_blocks - 1), 0)),
            pl.BlockSpec((None, MOD_ROWS, D_MODEL), lambda i: (_tile_kind(i), 0, 0)),
            _const_spec((SUBLANES, D_MODEL)),
            _const_spec((D_MODEL, 2 * D_RNN)),
            _const_spec((SUBLANES, D_RNN)),
        ],
        out_specs=[pl.BlockSpec((TM, D_RNN), lambda i: (i, 0)),
                   pl.BlockSpec((LRU_BLOCKS, TM, LRU_BW), lambda i: (0, i, 0))],
        out_shape=[jax.ShapeDtypeStruct((N_ALL, D_RNN), F32),
                   jax.ShapeDtypeStruct((LRU_BLOCKS, N_ALL, LRU_BW), F32)],
        scratch_shapes=[pltpu.VMEM((TM, D_MODEL), BF16)],
        compiler_params=_cparams(("arbitrary",)),
        name="lru_in_proj",
    )(xs, xs, xs, mods_i, ng_i, w, cw)


def _softplus(x):
    return jnp.maximum(x, 0.0) + jnp.log1p(jnp.exp(-jnp.abs(x)))


def _lru_kernel(xc_ref, gw_ref, gb_ref, o_ref, h_ref, p_ref, carry_ref, *, reverse):
    m = pl.program_id(1)

    @pl.when(m == 0)
    def _():
        carry_ref[...] = jnp.zeros_like(carry_ref)

    sub = lax.broadcasted_iota(jnp.int32, (SUBLANES, LRU_BW), 0)
    steps = range(LRU_SEG - 1, -1, -1) if reverse else range(LRU_SEG)
    segs = range(SUBLANES - 1, -1, -1) if reverse else range(SUBLANES)

    for n in range(LRU_BLOCKS):
        cols = slice(n * LRU_BW, (n + 1) * LRU_BW)
        xc = xc_ref[n]
        ri = jnp.dot(xc.astype(BF16), gw_ref[n], preferred_element_type=F32)
        r2 = jnp.tanh(ri[:, :LRU_BW] + gb_ref[0:1, cols]) + 1.0
        i2 = jnp.tanh(ri[:, LRU_BW:] + gb_ref[1:2, cols]) + 1.0
        log_a = r2 * ((-0.5 * LRU_C) * _softplus(-gb_ref[2:3, cols]))
        a = jnp.exp(log_a)
        t = jnp.tanh(log_a)
        z = (-0.5 * t) / (1.0 - t)
        b = jnp.where(z > 0.0, z * lax.rsqrt(z), 0.0) * (i2 * xc)

        h = jnp.zeros((SUBLANES, LRU_BW), F32)
        p = jnp.ones((SUBLANES, LRU_BW), F32)
        for k in steps:
            grp = slice(k * SUBLANES, (k + 1) * SUBLANES)
            h = a[grp] * h + b[grp]
            p = a[grp] * p
            h_ref[n, grp, :] = h
            p_ref[n, grp, :] = p
        st = carry_ref[n, 0:1, :]
        enter = jnp.zeros((SUBLANES, LRU_BW), F32)
        for s in segs:
            enter = jnp.where(sub == s, st, enter)
            st = p[s:s + 1, :] * st + h[s:s + 1, :]
        carry_ref[n, 0:1, :] = st
        for k in range(LRU_SEG):
            grp = slice(k * SUBLANES, (k + 1) * SUBLANES)
            o_ref[n, grp, :] = h_ref[n, grp, :] + p_ref[n, grp, :] * enter


def _lru_scan(xc, gw, gb, reverse):
    def row_block(b, m):
        c = (X_CHUNKS - m) if reverse else (m - 1)
        return jnp.where(m == 0, N_X // LRU_T + b, b * X_CHUNKS + c)

    cur = lambda b, m: (0, row_block(b, m), 0)
    return pl.pallas_call(
        functools.partial(_lru_kernel, reverse=reverse),
        grid=(BATCH, X_CHUNKS + 1),
        in_specs=[
            pl.BlockSpec((LRU_BLOCKS, LRU_T, LRU_BW), cur),
            _const_spec((LRU_BLOCKS, LRU_BW, 2 * LRU_BW)),
            _const_spec((SUBLANES, D_RNN)),
        ],
        out_specs=pl.BlockSpec((LRU_BLOCKS, LRU_T, LRU_BW), cur),
        out_shape=jax.ShapeDtypeStruct((LRU_BLOCKS, N_ALL, LRU_BW), F32),
        scratch_shapes=[
            pltpu.VMEM((LRU_BLOCKS, LRU_T, LRU_BW), F32),
            pltpu.VMEM((LRU_BLOCKS, LRU_T, LRU_BW), F32),
            pltpu.VMEM((LRU_BLOCKS, SUBLANES, LRU_BW), F32),
        ],
        compiler_params=_cparams(("arbitrary", "arbitrary")),
        name="lru_bwd" if reverse else "lru_fwd",
    )(xc, gw, gb)


SWA_TQ = 128
SWA_XQ = SEQ // SWA_TQ
SWA_CQ = CTX_LEN // SWA_TQ
SWA_SPAN = 3 * SWA_TQ
KV_COLS = 2 * B_KV_HEADS * LANES
_NT = (((1,), (1,)), ((), ()))


def _split_heads(qt):
    lo = lax.broadcasted_iota(jnp.int32, qt.shape, 1) < HEAD_DIM
    zero = jnp.zeros_like(qt)
    return jnp.concatenate([jnp.where(lo, qt, zero), jnp.where(lo, zero, qt)], axis=0)


def _merge_heads(o):
    t = o.shape[0] // 2
    lo = lax.broadcasted_iota(jnp.int32, (t, LANES), 1) < HEAD_DIM
    return jnp.where(lo, o[:t], o[t:])


def _swa_kernel(sink_ref, q_ref, kvp_ref, kvc_ref, kvn_ref, kvx_ref, o_ref):
    j = pl.program_id(1)
    iq = lax.broadcasted_iota(jnp.int32, (SWA_TQ, SWA_SPAN), 0)
    ik = lax.broadcasted_iota(jnp.int32, (SWA_TQ, SWA_SPAN), 1)
    lo_ok = jnp.where(j > 0, 0, SWA_TQ)
    hi_ok = jnp.where(j < SWA_XQ - 1, SWA_SPAN, 2 * SWA_TQ)
    hi_ok = jnp.where(j < SWA_XQ, hi_ok, 0)
    dist = iq + WINDOW - ik
    bias1 = jnp.where(jnp.abs(dist) <= WINDOW, 0.0, NEG)
    bias1 = jnp.where(ik >= lo_ok, bias1, NEG)
    bias1 = jnp.where(ik < hi_ok, bias1, NEG)
    bias = jnp.concatenate([bias1, bias1], axis=0)
    top = lax.broadcasted_iota(jnp.int32, (2 * SWA_TQ, 1), 0) < SWA_TQ
    for g in range(B_KV_HEADS):
        kcol = slice(g * LANES, (g + 1) * LANES)
        vcol = slice((B_KV_HEADS + g) * LANES, (B_KV_HEADS + g + 1) * LANES)
        kd = jnp.concatenate([kvp_ref[:, kcol], kvc_ref[:, kcol], kvn_ref[:, kcol]], axis=0)
        vd = jnp.concatenate([kvp_ref[:, vcol], kvc_ref[:, vcol], kvn_ref[:, vcol]], axis=0)
        kx = kvx_ref[:, kcol]
        vx = kvx_ref[:, vcol]
        for t in range(2):
            tile = 2 * g + t
            qs = _split_heads(q_ref[:, tile * LANES:(tile + 1) * LANES])
            s_loc = lax.dot_general(qs, kd, _NT, preferred_element_type=F32) + bias
            s_ctx = lax.dot_general(qs, kx, _NT, preferred_element_type=F32)
            sink = jnp.where(top, sink_ref[2 * tile], sink_ref[2 * tile + 1])
            mx = jnp.maximum(jnp.maximum(jnp.max(s_loc, axis=-1, keepdims=True),
                                         jnp.max(s_ctx, axis=-1, keepdims=True)), sink)
            p_loc = jnp.exp(s_loc - mx)
            p_ctx = jnp.exp(s_ctx - mx)
            den = (jnp.sum(p_loc, axis=-1, keepdims=True) + jnp.sum(p_ctx, axis=-1, keepdims=True)
                   + jnp.exp(sink - mx))
            o = (jnp.dot(p_loc.astype(BF16), vd, preferred_element_type=F32)
                 + jnp.dot(p_ctx.astype(BF16), vx, preferred_element_type=F32)) / den
            o_ref[:, tile * LANES:(tile + 1) * LANES] = _merge_heads(o).astype(BF16)


def _swa_attention(qkv, sinks):
    nq = SWA_XQ + SWA_CQ

    def q_block(b, j):
        return jnp.where(j < SWA_XQ, b * SWA_XQ + j, N_X // SWA_TQ + b * SWA_CQ + (j - SWA_XQ))

    def kv_block(off):
        return lambda b, j: (b * SWA_XQ + jnp.clip(j + off, 0, SWA_XQ - 1), 1)

    kv_spec = lambda off: pl.BlockSpec((SWA_TQ, KV_COLS), kv_block(off))
    return pl.pallas_call(
        _swa_kernel,
        grid=(BATCH, nq),
        in_specs=[
            pl.BlockSpec(memory_space=pltpu.SMEM),
            pl.BlockSpec((SWA_TQ, D_MODEL), lambda b, j: (q_block(b, j), 0)),
            kv_spec(-1), kv_spec(0), kv_spec(1),
            pl.BlockSpec((CTX_LEN, KV_COLS), lambda b, j: (N_X // CTX_LEN + b, 1)),
        ],
        out_specs=pl.BlockSpec((SWA_TQ, D_MODEL), lambda b, j: (q_block(b, j), 0)),
        out_shape=jax.ShapeDtypeStruct((N_ALL, D_MODEL), BF16),
        compiler_params=_cparams(("arbitrary", "arbitrary")),
        name="swa_attn",
    )(sinks, qkv, qkv, qkv, qkv, qkv)


def _prep_swa_weights(w_qkv):
    nq = B_HEADS * HEAD_DIM
    nk = B_KV_HEADS * HEAD_DIM
    wq = w_qkv[:, :nq] * (HEAD_DIM ** -0.5)
    dup = lambda w: jnp.tile(w.reshape(D_MODEL, B_KV_HEADS, 1, HEAD_DIM), (1, 1, 2, 1)).reshape(D_MODEL, -1)
    wk = dup(w_qkv[:, nq:nq + nk])
    wv = dup(w_qkv[:, nq + nk:])
    return jnp.concatenate([wq, wk, wv], axis=1).astype(BF16)


NAT_ROWS = 4
NAT_TQ = NAT_ROWS * GRID_W
NAT_XQ = SEQ // NAT_TQ
NAT_SPAN = 3 * NAT_TQ
N_HEAD_TILES = C_HEADS * HEAD_DIM // LANES
GRID_ROWS = SEQ // GRID_W


NAT_KEY_ROWS = NAT_SPAN // GRID_W
NAT_RPB_ROWS = 2 * WIN_H
NAT_PATTERNS = 4


def _nat_pattern(j):
    return jnp.where(j == 0, 0, jnp.where(j == NAT_XQ - 1, 2, jnp.where(j == NAT_XQ, 3, 1)))


def _nat_kernel(sel_ref, q_ref, kp_ref, kc_ref, kn_ref, vp_ref, vc_ref, vn_ref, kx_ref, vx_ref, tab_ref,
                o_ref, bias_ref):
    j = pl.program_id(1)

    @pl.when(jnp.logical_or(j <= 1, j >= NAT_XQ - 1))
    def _():
        lo = lax.broadcasted_iota(jnp.int32, (GRID_W, LANES), 1) < GRID_W
        base = _nat_pattern(j) * (NAT_ROWS * NAT_KEY_ROWS)
        for qr in range(NAT_ROWS):
            rows = slice(qr * GRID_W, (qr + 1) * GRID_W)
            for kp in range(NAT_KEY_ROWS // 2):
                e0 = sel_ref[base + qr * NAT_KEY_ROWS + 2 * kp]
                e1 = sel_ref[base + qr * NAT_KEY_ROWS + 2 * kp + 1]

                def fill(h, carry, rows=rows, kp=kp, e0=e0, e1=e1):
                    bias_ref[h, rows, kp * LANES:(kp + 1) * LANES] = jnp.where(lo, tab_ref[h, e0], tab_ref[h, e1])
                    return carry

                lax.fori_loop(0, C_HEADS, fill, 0)

    for t in range(N_HEAD_TILES):
        cols = slice(t * LANES, (t + 1) * LANES)
        qs = _split_heads(q_ref[:, cols])
        kd = jnp.concatenate([kp_ref[:, cols], kc_ref[:, cols], kn_ref[:, cols]], axis=0)
        vd = jnp.concatenate([vp_ref[:, cols], vc_ref[:, cols], vn_ref[:, cols]], axis=0)
        bias = bias_ref[2 * t:2 * t + 2].reshape(2 * NAT_TQ, NAT_SPAN)
        s_loc = lax.dot_general(qs, kd, _NT, preferred_element_type=F32) + bias
        s_ctx = lax.dot_general(qs, kx_ref[:, cols], _NT, preferred_element_type=F32)
        mx = jnp.maximum(jnp.max(s_loc, axis=-1, keepdims=True), jnp.max(s_ctx, axis=-1, keepdims=True))
        p_loc = jnp.exp(s_loc - mx)
        p_ctx = jnp.exp(s_ctx - mx)
        den = jnp.sum(p_loc, axis=-1, keepdims=True) + jnp.sum(p_ctx, axis=-1, keepdims=True)
        o = (jnp.dot(p_loc.astype(BF16), vd, preferred_element_type=F32)
             + jnp.dot(p_ctx.astype(BF16), vx_ref[:, cols], preferred_element_type=F32)) / den
        o_ref[:, cols] = _merge_heads(o).astype(BF16)


def _nat_attention(qkv, row_sel, tab):
    nq = NAT_XQ + 1

    def q_block(b, j):
        return jnp.where(j < NAT_XQ, b * NAT_XQ + j, N_X // NAT_TQ + b)

    def kv_spec(off, part):
        return pl.BlockSpec(
            (NAT_TQ, D_MODEL), lambda b, j: (b * NAT_XQ + jnp.clip(j + off, 0, NAT_XQ - 1), part))

    def ctx_spec(part):
        return pl.BlockSpec((CTX_LEN, D_MODEL), lambda b, j: (N_X // CTX_LEN + b, part))

    return pl.pallas_call(
        _nat_kernel,
        grid=(BATCH, nq),
        in_specs=[
            pl.BlockSpec(memory_space=pltpu.SMEM),
            pl.BlockSpec((NAT_TQ, D_MODEL), lambda b, j: (q_block(b, j), 0)),
            kv_spec(-1, 1), kv_spec(0, 1), kv_spec(1, 1),
            kv_spec(-1, 2), kv_spec(0, 2), kv_spec(1, 2),
            ctx_spec(1), ctx_spec(2),
            _const_spec((C_HEADS, NAT_RPB_ROWS, GRID_W, LANES)),
        ],
        out_specs=pl.BlockSpec((NAT_TQ, D_MODEL), lambda b, j: (q_block(b, j), 0)),
        out_shape=jax.ShapeDtypeStruct((N_ALL, D_MODEL), BF16),
        scratch_shapes=[pltpu.VMEM((C_HEADS, NAT_TQ, NAT_SPAN), F32)],
        compiler_params=_cparams(("arbitrary", "arbitrary")),
        name="nat_attn",
    )(row_sel, qkv, qkv, qkv, qkv, qkv, qkv, qkv, qkv, qkv, tab)


def _nat_row_select():
    qr = np.arange(NAT_ROWS)
    kr = np.arange(NAT_KEY_ROWS) - NAT_ROWS
    row_sel = []
    for r0 in (0, NAT_ROWS, GRID_ROWS - NAT_ROWS):
        r = r0 + qr
        rs = np.clip(r - WIN_H // 2, 0, GRID_ROWS - WIN_H)
        k_abs = r0 + kr
        row_ok = (k_abs[None, :] >= rs[:, None]) & (k_abs[None, :] < rs[:, None] + WIN_H)
        row_sel.append(np.where(row_ok, k_abs[None, :] - r[:, None] + (WIN_H - 1), NAT_RPB_ROWS - 1))
    row_sel.append(np.full((NAT_ROWS, NAT_KEY_ROWS), NAT_RPB_ROWS - 1))
    return jnp.asarray(np.stack(row_sel).reshape(-1), jnp.int32)


def _nat_bias_tiles(rpb):
    col = np.arange(GRID_W)
    col_start = np.clip(col - WIN_W // 2, 0, GRID_W - WIN_W)
    col_ok = (col[None, :] >= col_start[:, None]) & (col[None, :] < col_start[:, None] + WIN_W)
    col_off = np.clip(col[None, :] - col[:, None] + (WIN_W - 1), 0, 2 * WIN_W - 2)
    by_col = jnp.where(col_ok[None, None], rpb[:, :, col_off], NEG)
    by_col = jnp.concatenate([by_col, jnp.full((C_HEADS, 1, GRID_W, GRID_W), NEG, F32)], axis=1)
    return jnp.concatenate([by_col, by_col], axis=-1)


def _prep_nat_weights(w_qkv):
    nq = C_HEADS * HEAD_DIM
    return jnp.concatenate([w_qkv[:, :nq] * (HEAD_DIM ** -0.5), w_qkv[:, nq:]], axis=1).astype(BF16)


def _pad_rows(a, rows=SUBLANES):
    return jnp.pad(a, ((0, rows - a.shape[0]), (0, 0)))


def kernel(x, c, ctx, c_ctx, w_ada, b_ada, norm_g, w_ffn_gu, w_ffn_down, a_w_in, a_conv_w, a_conv_b,
           a_gate_w, a_gate_b, a_lambda, a_w_out, b_w_qkv, b_sinks, b_w_o, c_w_qkv, c_rpb, c_w_o, final_g):
    xs = x.reshape(N_X, D_MODEL)
    cvec = _pad_rows(jnp.concatenate([c, c_ctx[None, :]], axis=0))
    mods = _ada_mods(cvec, w_ada, b_ada)
    rope = _rope_tables()
    wgu, wd = w_ffn_gu.astype(BF16), w_ffn_down.astype(BF16)
    for i in range(DEPTH):
        kind, j = i % N_MIXERS, i // N_MIXERS
        last = i == DEPTH - 1
        ng = _pad_rows(norm_g[i])
        xs = _ffn(xs, mods[i], ng, wgu, wd, i, 0, 0, ctx=ctx.reshape(N_CTX, D_MODEL) if i == 0 else None)
        if kind == 0:
            cw = _pad_rows(jnp.concatenate([a_conv_w[j], a_conv_b[j][None, :]], axis=0))
            g_branch, xc = _lru_in_proj(xs, mods[i], ng, a_w_in[j].astype(BF16), cw)
            hs = []
            for d in range(2):
                gb = _pad_rows(jnp.concatenate([0.5 * a_gate_b[j, d], a_lambda[j, d][None, :]], axis=0))
                gw = (0.5 * jnp.concatenate([a_gate_w[j, d, 0], a_gate_w[j, d, 1]], axis=-1)).astype(BF16)
                hs.append(_lru_scan(xc, gw, gb, reverse=bool(d)))
            mixed = dict(lru=(hs[0], hs[1], g_branch), w_o=a_w_out[j].astype(BF16))
        elif kind == 1:
            qkv = _nmm(xs, mods[i], ng, _prep_swa_weights(b_w_qkv[j]), BF16, rope=rope,
                       rope_cols=B_HEADS * HEAD_DIM + B_KV_HEADS * LANES)
            mixed = dict(attn=_swa_attention(qkv, b_sinks[j]), w_o=b_w_o[j].astype(BF16))
        else:
            qkv = _nmm(xs, mods[i], ng, _prep_nat_weights(c_w_qkv[j]), BF16)
            o = _nat_attention(qkv, _nat_row_select(), _nat_bias_tiles(c_rpb[j]))
            mixed = dict(attn=o, w_o=c_w_o[j].astype(BF16))
        xs = _ffn(xs, mods[i], ng, wgu, wd, i, 1, 2, final_g=final_g if last else None, **mixed)
    return xs.reshape(BATCH, SEQ, D_MODEL)
```

```python
import functools

import numpy as np
import jax
import jax.numpy as jnp
from jax import lax
from jax.experimental import pallas as pl
from jax.experimental.pallas import tpu as pltpu

D_MODEL = 1024
BATCH = 2
SEQ = 16384
DEPTH = 4
GRID_W = 64
CTX_LEN = 256
N_MIXERS = 3
EPS = 1e-6
FFN_RES = 0.5
D_FF = 2816
D_RNN = 1536
LRU_BLOCKS = 12
LRU_BW = D_RNN // LRU_BLOCKS
CONV_W = 4
LRU_C = 8.0
HEAD_DIM = 64
B_HEADS = 16
B_KV_HEADS = 4
WINDOW = 128
ROPE_BASE = 10000.0
C_HEADS = 16
WIN_H = 8
WIN_W = 16
NEG = -1e30

LANES = 128
SUBLANES = 8
VMEM_LIMIT = 56 * 1024 * 1024

N_X = BATCH * SEQ
N_CTX = BATCH * CTX_LEN
N_ALL = N_X + N_CTX
TM = 512
N_TILES = N_ALL // TM
X_TILES = N_X // TM
TILES_PER_BATCH = SEQ // TM
MOD_ROWS = 16
FF_CHUNK = 256
N_FF_CHUNKS = D_FF // FF_CHUNK

F32 = jnp.float32
BF16 = jnp.bfloat16


def _cparams(sem):
    return pltpu.CompilerParams(dimension_semantics=sem, vmem_limit_bytes=VMEM_LIMIT)


def _const_spec(shape):
    nd = len(shape)
    return pl.BlockSpec(shape, lambda *_: (0,) * nd, pipeline_mode=pl.Buffered(1))


def _tile_kind(i):
    return jnp.minimum(i // TILES_PER_BATCH, BATCH)


def _adaln(x, mod_ref, ng_ref, sub):
    g = ng_ref[sub:sub + 1, :]
    shift = mod_ref[3 * sub:3 * sub + 1, :]
    scale = mod_ref[3 * sub + 1:3 * sub + 2, :]
    y = x * lax.rsqrt(jnp.mean(x * x, axis=-1, keepdims=True) + EPS)
    return (y * g) * (1.0 + scale) + shift


ADA_COLS = 1152


def _ada_kernel(c_ref, w_ref, b_ref, o_ref):
    c = c_ref[...]
    sc = c * jax.nn.sigmoid(c)
    o_ref[...] = jnp.dot(sc, w_ref[...], preferred_element_type=F32,
                         precision=lax.Precision.HIGHEST) + b_ref[...]


def _ada_mods(cvec, w_ada, b_ada):
    out = pl.pallas_call(
        _ada_kernel,
        grid=(DEPTH, 9 * D_MODEL // ADA_COLS),
        in_specs=[
            pl.BlockSpec((SUBLANES, D_MODEL), lambda i, j: (0, 0)),
            pl.BlockSpec((None, D_MODEL, ADA_COLS), lambda i, j: (i, 0, j)),
            pl.BlockSpec((None, 1, ADA_COLS), lambda i, j: (i, 0, j)),
        ],
        out_specs=pl.BlockSpec((None, SUBLANES, ADA_COLS), lambda i, j: (i, 0, j)),
        out_shape=jax.ShapeDtypeStruct((DEPTH, SUBLANES, 9 * D_MODEL), F32),
        compiler_params=_cparams(("arbitrary", "arbitrary")),
        name="ada_mods",
    )(cvec, w_ada, b_ada.reshape(DEPTH, 1, 9 * D_MODEL))
    mods = out.reshape(DEPTH, SUBLANES, 9, D_MODEL)[:, :BATCH + 1]
    return jnp.pad(mods, ((0, 0), (0, 0), (0, MOD_ROWS - 9), (0, 0)))


def _gelu_tanh(g):
    return 0.5 * g * (1.0 + jnp.tanh(np.sqrt(2.0 / np.pi).astype(np.float32) * (g + 0.044715 * (g * g * g))))


def _ffn_kernel(*refs, sub, mixer, final):
    x_ref, mod_ref, ng_ref, wgu_ref, wd_ref = refs[:5]
    rest = list(refs[5:])
    if mixer == "attn":
        a_ref, wo_ref = rest[:2]
        rest = rest[2:]
        mixed = a_ref[...]
    elif mixer == "lru":
        hf_ref, hbk_ref, g_ref, wo_ref = rest[:4]
        rest = rest[4:]
        h = jnp.concatenate(
            [jnp.concatenate([_from_scan_order(hf_ref[n, ch * LRU_T:(ch + 1) * LRU_T, :]
                                               + hbk_ref[n, ch * LRU_T:(ch + 1) * LRU_T, :])
                              for ch in range(CHUNKS_PER_TILE)], axis=0)
             for n in range(LRU_BLOCKS)], axis=1)
        mixed = (h * _gelu_tanh(g_ref[...])).astype(BF16)
    elif mixer == "join":
        ctx_ref = rest.pop(0)
    fg_ref = rest.pop(0) if final else None
    o_ref, hb_ref, acc_ref, xm_ref = rest
    if mixer is None:
        xm_ref = x_ref
    elif mixer == "join":
        xm_ref[...] = jnp.where(pl.program_id(0) < X_TILES, x_ref[...], ctx_ref[...])
    else:
        y = jnp.dot(mixed, wo_ref[...], preferred_element_type=F32)
        xm_ref[...] = x_ref[...] + mod_ref[5:6, :] * y
    hb_ref[...] = _adaln(xm_ref[...], mod_ref, ng_ref, sub).astype(BF16)
    acc_ref[...] = jnp.zeros_like(acc_ref)

    for c in range(N_FF_CHUNKS):
        lo = c * FF_CHUNK
        g = jnp.dot(hb_ref[...], wgu_ref[:, lo:lo + FF_CHUNK], preferred_element_type=F32)
        u = jnp.dot(hb_ref[...], wgu_ref[:, D_FF + lo:D_FF + lo + FF_CHUNK], preferred_element_type=F32)
        a = ((g * jax.nn.sigmoid(g)) * u).astype(BF16)
        acc_ref[...] += jnp.dot(a, wd_ref[lo:lo + FF_CHUNK, :], preferred_element_type=F32)
    gate = mod_ref[3 * sub + 2:3 * sub + 3, :]
    y = xm_ref[...] + (FFN_RES * gate) * acc_ref[...]
    if final:
        y = (y * lax.rsqrt(jnp.mean(y * y, axis=-1, keepdims=True) + EPS)) * fg_ref[...]
    o_ref[...] = y


def _ffn(xs, mods_i, ng_i, wgu, wd, layer, which, sub, ctx=None, attn=None, lru=None, w_o=None, final_g=None):
    final = final_g is not None
    n_tiles = X_TILES if final else N_TILES
    w_map = lambda i: (layer, which, 0, 0)
    row_spec = lambda cols: pl.BlockSpec((TM, cols), lambda i: (i, 0))
    x_spec = row_spec(D_MODEL)
    if ctx is not None:
        x_spec = pl.BlockSpec((TM, D_MODEL), lambda i: (jnp.minimum(i, X_TILES - 1), 0))
    in_specs = [
        x_spec,
        pl.BlockSpec((None, MOD_ROWS, D_MODEL), lambda i: (_tile_kind(i), 0, 0)),
        _const_spec((SUBLANES, D_MODEL)),
        pl.BlockSpec((None, None, D_MODEL, 2 * D_FF), w_map, pipeline_mode=pl.Buffered(1)),
        pl.BlockSpec((None, None, D_FF, D_MODEL), w_map, pipeline_mode=pl.Buffered(1)),
    ]
    args = [xs, mods_i, ng_i, wgu, wd]
    mixer = None
    if attn is not None:
        mixer = "attn"
        in_specs += [row_spec(D_MODEL), _const_spec((D_MODEL, D_MODEL))]
        args += [attn, w_o]
    elif lru is not None:
        mixer = "lru"
        h_spec = pl.BlockSpec((LRU_BLOCKS, TM, LRU_BW), lambda i: (0, i, 0))
        in_specs += [h_spec, h_spec, row_spec(D_RNN), _const_spec((D_RNN, D_MODEL))]
        args += [*lru, w_o]
    elif ctx is not None:
        mixer = "join"
        in_specs.append(_const_spec((N_CTX, D_MODEL)))
        args.append(ctx)
    if final:
        in_specs.append(_const_spec((1, D_MODEL)))
        args.append(final_g.reshape(1, D_MODEL))
    scratch = [pltpu.VMEM((TM, D_MODEL), BF16), pltpu.VMEM((TM, D_MODEL), F32), pltpu.VMEM((TM, D_MODEL), F32)]
    return pl.pallas_call(
        functools.partial(_ffn_kernel, sub=sub, mixer=mixer, final=final),
        grid=(n_tiles,),
        in_specs=in_specs,
        out_specs=row_spec(D_MODEL),
        out_shape=jax.ShapeDtypeStruct((n_tiles * TM, D_MODEL), F32),
        scratch_shapes=scratch,
        compiler_params=_cparams(("arbitrary",)),
        name="ffn_" + (mixer or "plain") + ("_final" if final else ""),
    )(*args)


NMM_COLS = 512


def _rope_tile(y, c, s1, s2):
    return y * c + pltpu.roll(y, LANES - 16, 1) * s1 + pltpu.roll(y, 16, 1) * s2


def _nmm_kernel(*refs, n_cols, rope_cols, out_dtype):
    if rope_cols:
        x_ref, mod_ref, ng_ref, w_ref, c_ref, s1_ref, s2_ref, o_ref = refs
    else:
        x_ref, mod_ref, ng_ref, w_ref, o_ref = refs
    hb = _adaln(x_ref[...], mod_ref, ng_ref, 1).astype(BF16)
    for c0 in range(0, n_cols, NMM_COLS):
        y = jnp.dot(hb, w_ref[:, c0:c0 + NMM_COLS], preferred_element_type=F32)
        if c0 < rope_cols:
            c, s1, s2 = c_ref[...], s1_ref[...], s2_ref[...]
            y = jnp.concatenate(
                [_rope_tile(y[:, t:t + LANES], c, s1, s2) for t in range(0, NMM_COLS, LANES)], axis=1)
        o_ref[:, c0:c0 + NMM_COLS] = y.astype(out_dtype)


def _nmm(xs, mods_i, ng_i, w, out_dtype, rope=None, rope_cols=0):
    n_cols = w.shape[1]
    in_specs = [
        pl.BlockSpec((TM, D_MODEL), lambda i: (i, 0)),
        pl.BlockSpec((None, MOD_ROWS, D_MODEL), lambda i: (_tile_kind(i), 0, 0)),
        _const_spec((SUBLANES, D_MODEL)),
        _const_spec((D_MODEL, n_cols)),
    ]
    args = [xs, mods_i, ng_i, w]
    if rope_cols:
        rmap = lambda i: (jnp.where(i < X_TILES, i % TILES_PER_BATCH, TILES_PER_BATCH), 0)
        in_specs += [pl.BlockSpec((TM, LANES), rmap)] * 3
        args += list(rope)
    return pl.pallas_call(
        functools.partial(_nmm_kernel, n_cols=n_cols, rope_cols=rope_cols, out_dtype=out_dtype),
        grid=(N_TILES,),
        in_specs=in_specs,
        out_specs=pl.BlockSpec((TM, n_cols), lambda i: (i, 0)),
        out_shape=jax.ShapeDtypeStruct((N_ALL, n_cols), out_dtype),
        compiler_params=_cparams(("arbitrary",)),
        name="adaln_proj",
    )(*args)


def _rope_tables():
    quarter = HEAD_DIM // 4
    pos = jnp.arange(SEQ)
    inv = ROPE_BASE ** (-jnp.arange(quarter, dtype=F32) / quarter)
    lane = np.arange(LANES)
    d = lane % HEAD_DIM
    use_col = (d >= HEAD_DIM // 2)
    first = (d % (HEAD_DIM // 2)) < quarter
    ang_row = (pos // GRID_W).astype(F32)[:, None] * inv
    ang_col = (pos % GRID_W).astype(F32)[:, None] * inv
    fi = d % quarter
    ang = jnp.where(use_col[None, :], ang_col[:, fi], ang_row[:, fi])
    cos, sin = jnp.cos(ang), jnp.sin(ang)
    s1 = jnp.where(first[None, :], -sin, 0.0)
    s2 = jnp.where(first[None, :], 0.0, sin)
    ident = jnp.ones((TM, LANES), F32)
    zero = jnp.zeros((TM, LANES), F32)
    return (jnp.concatenate([cos, ident]), jnp.concatenate([s1, zero]), jnp.concatenate([s2, zero]))


LRU_T = 256
LRU_SEG = LRU_T // SUBLANES
X_CHUNKS = SEQ // LRU_T
HALO = SUBLANES
CHUNKS_PER_TILE = TM // LRU_T
PROJ_GROUP = 256


def _to_scan_order(v):
    return jnp.swapaxes(v.reshape(SUBLANES, LRU_SEG, LRU_BW), 0, 1).reshape(LRU_T, LRU_BW)


def _from_scan_order(v):
    return jnp.swapaxes(v.reshape(LRU_SEG, SUBLANES, LRU_BW), 0, 1).reshape(LRU_T, LRU_BW)


def _lru_in_kernel(x_ref, xp_ref, xn_ref, mod_ref, ng_ref, w_ref, cw_ref, g_ref, xc_ref, hb_ref):
    i = pl.program_id(0)
    is_ctx = i >= X_TILES
    first = i % TILES_PER_BATCH == 0
    last = i % TILES_PER_BATCH == TILES_PER_BATCH - 1
    hb_ref[...] = _adaln(x_ref[...], mod_ref, ng_ref, 1).astype(BF16)
    halo = jnp.concatenate([xp_ref[...], xn_ref[...]], axis=0)
    hh = _adaln(halo, mod_ref, ng_ref, 1).astype(BF16)
    sub = lax.broadcasted_iota(jnp.int32, (SUBLANES, LRU_BW), 0)
    pad_before = [jnp.logical_or(is_ctx, first) if ch == 0 else is_ctx for ch in range(CHUNKS_PER_TILE)]
    pad_after = [jnp.logical_or(is_ctx, last) if ch == CHUNKS_PER_TILE - 1 else is_ctx
                 for ch in range(CHUNKS_PER_TILE)]

    def conv(n, ch, chunk, before, after):
        cols = slice(n * LRU_BW, (n + 1) * LRU_BW)
        cb = cw_ref[CONV_W:CONV_W + 1, cols]
        taps = [cw_ref[j:j + 1, cols] for j in range(CONV_W)]
        before = jnp.where(pad_before[ch], 0.0, before)
        after = jnp.where(pad_after[ch], 0.0, after)
        xp = _to_scan_order(chunk)
        xg = [xp[k * SUBLANES:(k + 1) * SUBLANES] for k in range(LRU_SEG)]
        xm1 = jnp.where(sub == 0, before, pltpu.roll(xg[LRU_SEG - 1], 1, 0))
        xp0 = jnp.where(sub == SUBLANES - 1, after[0:1], pltpu.roll(xg[0], SUBLANES - 1, 0))
        xp1 = jnp.where(sub == SUBLANES - 1, after[1:2], pltpu.roll(xg[1], SUBLANES - 1, 0))
        xe = [xm1] + xg + [xp0, xp1]
        for k in range(LRU_SEG):
            xc = cb + xe[k] * taps[0] + xe[k + 1] * taps[1] + xe[k + 2] * taps[2] + xe[k + 3] * taps[3]
            xc_ref[n, ch * LRU_T + k * SUBLANES:ch * LRU_T + (k + 1) * SUBLANES, :] = xc

    for c0 in range(0, D_RNN, PROJ_GROUP):
        wx = w_ref[:, D_RNN + c0:D_RNN + c0 + PROJ_GROUP]
        y = jnp.dot(hb_ref[...], wx, preferred_element_type=F32)
        yh = jnp.dot(hh, wx, preferred_element_type=F32)
        g_ref[:, c0:c0 + PROJ_GROUP] = jnp.dot(hb_ref[...], w_ref[:, c0:c0 + PROJ_GROUP],
                                               preferred_element_type=F32)
        for t in range(PROJ_GROUP // LRU_BW):
            lanes = slice(t * LRU_BW, (t + 1) * LRU_BW)
            ext = jnp.concatenate([yh[0:HALO, lanes], y[:, lanes], yh[HALO:, lanes]], axis=0)
            for ch in range(CHUNKS_PER_TILE):
                lo = HALO + ch * LRU_T
                conv(c0 // LRU_BW + t, ch, ext[lo:lo + LRU_T], ext[lo - 1:lo], ext[lo + LRU_T:lo + LRU_T + 2])


def _lru_in_proj(xs, mods_i, ng_i, w, cw):
    blocks_per_tile = TM // HALO
    n_halo_blocks = N_ALL // HALO
    return pl.pallas_call(
        _lru_in_kernel,
        grid=(N_TILES,),
        in_specs=[
            pl.BlockSpec((TM, D_MODEL), lambda i: (i, 0)),
            pl.BlockSpec((HALO, D_MODEL), lambda i: (jnp.maximum(i * blocks_per_tile - 1, 0), 0)),
            pl.BlockSpec((HALO, D_MODEL), lambda i: (jnp.minimum((i + 1) * blocks_per_tile, n_halo_blocks - 1), 0)),
            pl.BlockSpec((None, MOD_ROWS, D_MODEL), lambda i: (_tile_kind(i), 0, 0)),
            _const_spec((SUBLANES, D_MODEL)),
            _const_spec((D_MODEL, 2 * D_RNN)),
            _const_spec((SUBLANES, D_RNN)),
        ],
        out_specs=[pl.BlockSpec((TM, D_RNN), lambda i: (i, 0)),
                   pl.BlockSpec((LRU_BLOCKS, TM, LRU_BW), lambda i: (0, i, 0))],
        out_shape=[jax.ShapeDtypeStruct((N_ALL, D_RNN), F32),
                   jax.ShapeDtypeStruct((LRU_BLOCKS, N_ALL, LRU_BW), F32)],
        scratch_shapes=[pltpu.VMEM((TM, D_MODEL), BF16)],
        compiler_params=_cparams(("arbitrary",)),
        name="lru_in_proj",
    )(xs, xs, xs, mods_i, ng_i, w, cw)


def _softplus(x):
    return jnp.maximum(x, 0.0) + jnp.log1p(jnp.exp(-jnp.abs(x)))


def _lru_kernel(xf_ref, xb_ref, gwf_ref, gwb_ref, gbf_ref, gbb_ref, of_ref, ob_ref,
                h_ref, p_ref, carry_ref):
    @pl.when(pl.program_id(1) == 0)
    def _():
        carry_ref[...] = jnp.zeros_like(carry_ref)

    for n in range(LRU_BLOCKS):
        _lru_block(n, xf_ref, gwf_ref, gbf_ref, of_ref, h_ref.at[0], p_ref.at[0], carry_ref.at[0], False)
        _lru_block(n, xb_ref, gwb_ref, gbb_ref, ob_ref, h_ref.at[1], p_ref.at[1], carry_ref.at[1], True)


def _lru_block(n, xc_ref, gw_ref, gb_ref, o_ref, h_ref, p_ref, carry_ref, reverse):
    sub = lax.broadcasted_iota(jnp.int32, (SUBLANES, LRU_BW), 0)
    steps = range(LRU_SEG - 1, -1, -1) if reverse else range(LRU_SEG)
    segs = range(SUBLANES - 1, -1, -1) if reverse else range(SUBLANES)

    cols = slice(n * LRU_BW, (n + 1) * LRU_BW)
    xc = xc_ref[n]
    ri = jnp.dot(xc.astype(BF16), gw_ref[n], preferred_element_type=F32)
    r2 = jnp.tanh(ri[:, :LRU_BW] + gb_ref[0:1, cols]) + 1.0
    i2 = jnp.tanh(ri[:, LRU_BW:] + gb_ref[1:2, cols]) + 1.0
    log_a = r2 * ((-0.5 * LRU_C) * _softplus(-gb_ref[2:3, cols]))
    a = jnp.exp(log_a)
    t = jnp.tanh(log_a)
    z = (-0.5 * t) / (1.0 - t)
    b = jnp.where(z > 0.0, z * lax.rsqrt(z), 0.0) * (i2 * xc)

    h = jnp.zeros((SUBLANES, LRU_BW), F32)
    p = jnp.ones((SUBLANES, LRU_BW), F32)
    for k in steps:
        grp = slice(k * SUBLANES, (k + 1) * SUBLANES)
        h = a[grp] * h + b[grp]
        p = a[grp] * p
        h_ref[n, grp, :] = h
        p_ref[n, grp, :] = p
    st = carry_ref[n, 0:1, :]
    enter = jnp.zeros((SUBLANES, LRU_BW), F32)
    for s in segs:
        enter = jnp.where(sub == s, st, enter)
        st = p[s:s + 1, :] * st + h[s:s + 1, :]
    carry_ref[n, 0:1, :] = st
    for k in range(LRU_SEG):
        grp = slice(k * SUBLANES, (k + 1) * SUBLANES)
        o_ref[n, grp, :] = h_ref[n, grp, :] + p_ref[n, grp, :] * enter


def _lru_scan(xc, gw, gb):
    def row_block(reverse):
        def f(b, m):
            c = (X_CHUNKS - m) if reverse else (m - 1)
            return (0, jnp.where(m == 0, N_X // LRU_T + b, b * X_CHUNKS + c), 0)
        return f

    chunk = lambda reverse: pl.BlockSpec((LRU_BLOCKS, LRU_T, LRU_BW), row_block(reverse))
    gw_spec = _const_spec((LRU_BLOCKS, LRU_BW, 2 * LRU_BW))
    gb_spec = _const_spec((SUBLANES, D_RNN))
    out = jax.ShapeDtypeStruct((LRU_BLOCKS, N_ALL, LRU_BW), F32)
    return pl.pallas_call(
        _lru_kernel,
        grid=(BATCH, X_CHUNKS + 1),
        in_specs=[chunk(False), chunk(True), gw_spec, gw_spec, gb_spec, gb_spec],
        out_specs=[chunk(False), chunk(True)],
        out_shape=[out, out],
        scratch_shapes=[
            pltpu.VMEM((2, LRU_BLOCKS, LRU_T, LRU_BW), F32),
            pltpu.VMEM((2, LRU_BLOCKS, LRU_T, LRU_BW), F32),
            pltpu.VMEM((2, LRU_BLOCKS, SUBLANES, LRU_BW), F32),
        ],
        compiler_params=_cparams(("arbitrary", "arbitrary")),
        name="lru_scan",
    )(xc, xc, gw[0], gw[1], gb[0], gb[1])


SWA_TQ = 128
SWA_XQ = SEQ // SWA_TQ
SWA_CQ = CTX_LEN // SWA_TQ
SWA_SPAN = 3 * SWA_TQ
KV_COLS = 2 * B_KV_HEADS * LANES
_NT = (((1,), (1,)), ((), ()))


def _split_heads(qt):
    lo = lax.broadcasted_iota(jnp.int32, qt.shape, 1) < HEAD_DIM
    zero = jnp.zeros_like(qt)
    return jnp.concatenate([jnp.where(lo, qt, zero), jnp.where(lo, zero, qt)], axis=0)


def _merge_heads(o):
    t = o.shape[0] // 2
    lo = lax.broadcasted_iota(jnp.int32, (t, LANES), 1) < HEAD_DIM
    return jnp.where(lo, o[:t], o[t:])


def _swa_kernel(sink_ref, q_ref, kvp_ref, kvc_ref, kvn_ref, kvx_ref, o_ref):
    j = pl.program_id(1)
    iq = lax.broadcasted_iota(jnp.int32, (SWA_TQ, SWA_SPAN), 0)
    ik = lax.broadcasted_iota(jnp.int32, (SWA_TQ, SWA_SPAN), 1)
    lo_ok = jnp.where(j > 0, 0, SWA_TQ)
    hi_ok = jnp.where(j < SWA_XQ - 1, SWA_SPAN, 2 * SWA_TQ)
    hi_ok = jnp.where(j < SWA_XQ, hi_ok, 0)
    dist = iq + WINDOW - ik
    bias1 = jnp.where(jnp.abs(dist) <= WINDOW, 0.0, NEG)
    bias1 = jnp.where(ik >= lo_ok, bias1, NEG)
    bias1 = jnp.where(ik < hi_ok, bias1, NEG)
    bias = jnp.concatenate([bias1, bias1], axis=0)
    top = lax.broadcasted_iota(jnp.int32, (2 * SWA_TQ, 1), 0) < SWA_TQ
    for g in range(B_KV_HEADS):
        kcol = slice(g * LANES, (g + 1) * LANES)
        vcol = slice((B_KV_HEADS + g) * LANES, (B_KV_HEADS + g + 1) * LANES)
        kd = jnp.concatenate([kvp_ref[:, kcol], kvc_ref[:, kcol], kvn_ref[:, kcol]], axis=0)
        vd = jnp.concatenate([kvp_ref[:, vcol], kvc_ref[:, vcol], kvn_ref[:, vcol]], axis=0)
        kx = kvx_ref[:, kcol]
        vx = kvx_ref[:, vcol]
        for t in range(2):
            tile = 2 * g + t
            qs = _split_heads(q_ref[:, tile * LANES:(tile + 1) * LANES])
            s_loc = lax.dot_general(qs, kd, _NT, preferred_element_type=F32) + bias
            s_ctx = lax.dot_general(qs, kx, _NT, preferred_element_type=F32)
            sink = jnp.where(top, sink_ref[2 * tile], sink_ref[2 * tile + 1])
            mx = jnp.maximum(jnp.maximum(jnp.max(s_loc, axis=-1, keepdims=True),
                                         jnp.max(s_ctx, axis=-1, keepdims=True)), sink)
            p_loc = jnp.exp(s_loc - mx)
            p_ctx = jnp.exp(s_ctx - mx)
            den = (jnp.sum(p_loc, axis=-1, keepdims=True) + jnp.sum(p_ctx, axis=-1, keepdims=True)
                   + jnp.exp(sink - mx))
            o = (jnp.dot(p_loc.astype(BF16), vd, preferred_element_type=F32)
                 + jnp.dot(p_ctx.astype(BF16), vx, preferred_element_type=F32)) / den
            o_ref[:, tile * LANES:(tile + 1) * LANES] = _merge_heads(o).astype(BF16)


def _swa_attention(qkv, sinks):
    nq = SWA_XQ + SWA_CQ

    def q_block(b, j):
        return jnp.where(j < SWA_XQ, b * SWA_XQ + j, N_X // SWA_TQ + b * SWA_CQ + (j - SWA_XQ))

    def kv_block(off):
        return lambda b, j: (b * SWA_XQ + jnp.clip(j + off, 0, SWA_XQ - 1), 1)

    kv_spec = lambda off: pl.BlockSpec((SWA_TQ, KV_COLS), kv_block(off))
    return pl.pallas_call(
        _swa_kernel,
        grid=(BATCH, nq),
        in_specs=[
            pl.BlockSpec(memory_space=pltpu.SMEM),
            pl.BlockSpec((SWA_TQ, D_MODEL), lambda b, j: (q_block(b, j), 0)),
            kv_spec(-1), kv_spec(0), kv_spec(1),
            pl.BlockSpec((CTX_LEN, KV_COLS), lambda b, j: (N_X // CTX_LEN + b, 1)),
        ],
        out_specs=pl.BlockSpec((SWA_TQ, D_MODEL), lambda b, j: (q_block(b, j), 0)),
        out_shape=jax.ShapeDtypeStruct((N_ALL, D_MODEL), BF16),
        compiler_params=_cparams(("arbitrary", "arbitrary")),
        name="swa_attn",
    )(sinks, qkv, qkv, qkv, qkv, qkv)


def _prep_swa_weights(w_qkv):
    nq = B_HEADS * HEAD_DIM
    nk = B_KV_HEADS * HEAD_DIM
    wq = w_qkv[:, :nq] * (HEAD_DIM ** -0.5)
    dup = lambda w: jnp.tile(w.reshape(D_MODEL, B_KV_HEADS, 1, HEAD_DIM), (1, 1, 2, 1)).reshape(D_MODEL, -1)
    wk = dup(w_qkv[:, nq:nq + nk])
    wv = dup(w_qkv[:, nq + nk:])
    return jnp.concatenate([wq, wk, wv], axis=1).astype(BF16)


NAT_ROWS = 4
NAT_TQ = NAT_ROWS * GRID_W
NAT_XQ = SEQ // NAT_TQ
NAT_SPAN = 3 * NAT_TQ
N_HEAD_TILES = C_HEADS * HEAD_DIM // LANES
GRID_ROWS = SEQ // GRID_W


NAT_KEY_ROWS = NAT_SPAN // GRID_W
NAT_RPB_ROWS = 2 * WIN_H
NAT_PATTERNS = 4


def _nat_pattern(j):
    return jnp.where(j == 0, 0, jnp.where(j == NAT_XQ - 1, 2, jnp.where(j == NAT_XQ, 3, 1)))


def _nat_kernel(sel_ref, q_ref, kp_ref, kc_ref, kn_ref, vp_ref, vc_ref, vn_ref, kx_ref, vx_ref, tab_ref,
                o_ref, bias_ref):
    j = pl.program_id(1)

    @pl.when(jnp.logical_or(j <= 1, j >= NAT_XQ - 1))
    def _():
        lo = lax.broadcasted_iota(jnp.int32, (GRID_W, LANES), 1) < GRID_W
        base = _nat_pattern(j) * (NAT_ROWS * NAT_KEY_ROWS)
        for qr in range(NAT_ROWS):
            rows = slice(qr * GRID_W, (qr + 1) * GRID_W)
            for kp in range(NAT_KEY_ROWS // 2):
                e0 = sel_ref[base + qr * NAT_KEY_ROWS + 2 * kp]
                e1 = sel_ref[base + qr * NAT_KEY_ROWS + 2 * kp + 1]

                def fill(h, carry, rows=rows, kp=kp, e0=e0, e1=e1):
                    bias_ref[h, rows, kp * LANES:(kp + 1) * LANES] = jnp.where(lo, tab_ref[h, e0], tab_ref[h, e1])
                    return carry

                lax.fori_loop(0, C_HEADS, fill, 0)

    for t in range(N_HEAD_TILES):
        cols = slice(t * LANES, (t + 1) * LANES)
        qs = _split_heads(q_ref[:, cols])
        kd = jnp.concatenate([kp_ref[:, cols], kc_ref[:, cols], kn_ref[:, cols]], axis=0)
        vd = jnp.concatenate([vp_ref[:, cols], vc_ref[:, cols], vn_ref[:, cols]], axis=0)
        bias = bias_ref[2 * t:2 * t + 2].reshape(2 * NAT_TQ, NAT_SPAN)
        s_loc = lax.dot_general(qs, kd, _NT, preferred_element_type=F32) + bias
        s_ctx = lax.dot_general(qs, kx_ref[:, cols], _NT, preferred_element_type=F32)
        mx = jnp.maximum(jnp.max(s_loc, axis=-1, keepdims=True), jnp.max(s_ctx, axis=-1, keepdims=True))
        p_loc = jnp.exp(s_loc - mx)
        p_ctx = jnp.exp(s_ctx - mx)
        den = jnp.sum(p_loc, axis=-1, keepdims=True) + jnp.sum(p_ctx, axis=-1, keepdims=True)
        o = (jnp.dot(p_loc.astype(BF16), vd, preferred_element_type=F32)
             + jnp.dot(p_ctx.astype(BF16), vx_ref[:, cols], preferred_element_type=F32)) / den
        o_ref[:, cols] = _merge_heads(o).astype(BF16)


def _nat_attention(qkv, row_sel, tab):
    nq = NAT_XQ + 1

    def q_block(b, j):
        return jnp.where(j < NAT_XQ, b * NAT_XQ + j, N_X // NAT_TQ + b)

    def kv_spec(off, part):
        return pl.BlockSpec(
            (NAT_TQ, D_MODEL), lambda b, j: (b * NAT_XQ + jnp.clip(j + off, 0, NAT_XQ - 1), part))

    def ctx_spec(part):
        return pl.BlockSpec((CTX_LEN, D_MODEL), lambda b, j: (N_X // CTX_LEN + b, part))

    return pl.pallas_call(
        _nat_kernel,
        grid=(BATCH, nq),
        in_specs=[
            pl.BlockSpec(memory_space=pltpu.SMEM),
            pl.BlockSpec((NAT_TQ, D_MODEL), lambda b, j: (q_block(b, j), 0)),
            kv_spec(-1, 1), kv_spec(0, 1), kv_spec(1, 1),
            kv_spec(-1, 2), kv_spec(0, 2), kv_spec(1, 2),
            ctx_spec(1), ctx_spec(2),
            _const_spec((C_HEADS, NAT_RPB_ROWS, GRID_W, LANES)),
        ],
        out_specs=pl.BlockSpec((NAT_TQ, D_MODEL), lambda b, j: (q_block(b, j), 0)),
        out_shape=jax.ShapeDtypeStruct((N_ALL, D_MODEL), BF16),
        scratch_shapes=[pltpu.VMEM((C_HEADS, NAT_TQ, NAT_SPAN), F32)],
        compiler_params=_cparams(("arbitrary", "arbitrary")),
        name="nat_attn",
    )(row_sel, qkv, qkv, qkv, qkv, qkv, qkv, qkv, qkv, qkv, tab)


def _nat_row_select():
    qr = np.arange(NAT_ROWS)
    kr = np.arange(NAT_KEY_ROWS) - NAT_ROWS
    row_sel = []
    for r0 in (0, NAT_ROWS, GRID_ROWS - NAT_ROWS):
        r = r0 + qr
        rs = np.clip(r - WIN_H // 2, 0, GRID_ROWS - WIN_H)
        k_abs = r0 + kr
        row_ok = (k_abs[None, :] >= rs[:, None]) & (k_abs[None, :] < rs[:, None] + WIN_H)
        row_sel.append(np.where(row_ok, k_abs[None, :] - r[:, None] + (WIN_H - 1), NAT_RPB_ROWS - 1))
    row_sel.append(np.full((NAT_ROWS, NAT_KEY_ROWS), NAT_RPB_ROWS - 1))
    return jnp.asarray(np.stack(row_sel).reshape(-1), jnp.int32)


def _nat_bias_tiles(rpb):
    col = np.arange(GRID_W)
    col_start = np.clip(col - WIN_W // 2, 0, GRID_W - WIN_W)
    col_ok = (col[None, :] >= col_start[:, None]) & (col[None, :] < col_start[:, None] + WIN_W)
    col_off = np.clip(col[None, :] - col[:, None] + (WIN_W - 1), 0, 2 * WIN_W - 2)
    by_col = jnp.where(col_ok[None, None], rpb[:, :, col_off], NEG)
    by_col = jnp.concatenate([by_col, jnp.full((C_HEADS, 1, GRID_W, GRID_W), NEG, F32)], axis=1)
    return jnp.concatenate([by_col, by_col], axis=-1)


def _prep_nat_weights(w_qkv):
    nq = C_HEADS * HEAD_DIM
    return jnp.concatenate([w_qkv[:, :nq] * (HEAD_DIM ** -0.5), w_qkv[:, nq:]], axis=1).astype(BF16)


def _pad_rows(a, rows=SUBLANES):
    return jnp.pad(a, ((0, rows - a.shape[0]), (0, 0)))


def kernel(x, c, ctx, c_ctx, w_ada, b_ada, norm_g, w_ffn_gu, w_ffn_down, a_w_in, a_conv_w, a_conv_b,
           a_gate_w, a_gate_b, a_lambda, a_w_out, b_w_qkv, b_sinks, b_w_o, c_w_qkv, c_rpb, c_w_o, final_g):
    xs = x.reshape(N_X, D_MODEL)
    cvec = _pad_rows(jnp.concatenate([c, c_ctx[None, :]], axis=0))
    mods = _ada_mods(cvec, w_ada, b_ada)
    rope = _rope_tables()
    wgu, wd = w_ffn_gu.astype(BF16), w_ffn_down.astype(BF16)
    for i in range(DEPTH):
        kind, j = i % N_MIXERS, i // N_MIXERS
        last = i == DEPTH - 1
        ng = _pad_rows(norm_g[i])
        xs = _ffn(xs, mods[i], ng, wgu, wd, i, 0, 0, ctx=ctx.reshape(N_CTX, D_MODEL) if i == 0 else None)
        if kind == 0:
            cw = _pad_rows(jnp.concatenate([a_conv_w[j], a_conv_b[j][None, :]], axis=0))
            g_branch, xc = _lru_in_proj(xs, mods[i], ng, a_w_in[j].astype(BF16), cw)
            gb = [_pad_rows(jnp.concatenate([0.5 * a_gate_b[j, d], a_lambda[j, d][None, :]], axis=0))
                  for d in range(2)]
            gw = [(0.5 * jnp.concatenate([a_gate_w[j, d, 0], a_gate_w[j, d, 1]], axis=-1)).astype(BF16)
                  for d in range(2)]
            hf, hb = _lru_scan(xc, gw, gb)
            mixed = dict(lru=(hf, hb, g_branch), w_o=a_w_out[j].astype(BF16))
        elif kind == 1:
            qkv = _nmm(xs, mods[i], ng, _prep_swa_weights(b_w_qkv[j]), BF16, rope=rope,
                       rope_cols=B_HEADS * HEAD_DIM + B_KV_HEADS * LANES)
            mixed = dict(attn=_swa_attention(qkv, b_sinks[j]), w_o=b_w_o[j].astype(BF16))
        else:
            qkv = _nmm(xs, mods[i], ng, _prep_nat_weights(c_w_qkv[j]), BF16)
            o = _nat_attention(qkv, _nat_row_select(), _nat_bias_tiles(c_rpb[j]))
            mixed = dict(attn=o, w_o=c_w_o[j].astype(BF16))
        xs = _ffn(xs, mods[i], ng, wgu, wd, i, 1, 2, final_g=final_g if last else None, **mixed)
    return xs.reshape(BATCH, SEQ, D_MODEL)
```

```python
import functools

import numpy as np
import jax
import jax.numpy as jnp
from jax import lax
from jax.experimental import pallas as pl
from jax.experimental.pallas import tpu as pltpu

D_MODEL = 1024
BATCH = 2
SEQ = 16384
DEPTH = 4
GRID_W = 64
CTX_LEN = 256
N_MIXERS = 3
EPS = 1e-6
FFN_RES = 0.5
D_FF = 2816
D_RNN = 1536
LRU_BLOCKS = 12
LRU_BW = D_RNN // LRU_BLOCKS
CONV_W = 4
LRU_C = 8.0
HEAD_DIM = 64
B_HEADS = 16
B_KV_HEADS = 4
WINDOW = 128
ROPE_BASE = 10000.0
C_HEADS = 16
WIN_H = 8
WIN_W = 16
NEG = -1e30

LANES = 128
SUBLANES = 8
VMEM_LIMIT = 56 * 1024 * 1024

N_X = BATCH * SEQ
N_CTX = BATCH * CTX_LEN
N_ALL = N_X + N_CTX
TM = 512
N_TILES = N_ALL // TM
X_TILES = N_X // TM
TILES_PER_BATCH = SEQ // TM
MOD_ROWS = 16
FF_CHUNK = 256
N_FF_CHUNKS = D_FF // FF_CHUNK

F32 = jnp.float32
BF16 = jnp.bfloat16


def _cparams(sem):
    return pltpu.CompilerParams(dimension_semantics=sem, vmem_limit_bytes=VMEM_LIMIT)


def _const_spec(shape):
    nd = len(shape)
    return pl.BlockSpec(shape, lambda *_: (0,) * nd, pipeline_mode=pl.Buffered(1))


def _tile_kind(i):
    return jnp.minimum(i // TILES_PER_BATCH, BATCH)


def _adaln(x, mod_ref, ng_ref, sub):
    g = ng_ref[sub:sub + 1, :]
    shift = mod_ref[3 * sub:3 * sub + 1, :]
    scale = mod_ref[3 * sub + 1:3 * sub + 2, :]
    y = x * lax.rsqrt(jnp.mean(x * x, axis=-1, keepdims=True) + EPS)
    return (y * g) * (1.0 + scale) + shift


ADA_COLS = 1152


def _ada_kernel(c_ref, w_ref, b_ref, o_ref):
    c = c_ref[...]
    sc = c * jax.nn.sigmoid(c)
    o_ref[...] = jnp.dot(sc, w_ref[...], preferred_element_type=F32,
                         precision=lax.Precision.HIGHEST) + b_ref[...]


def _ada_mods(cvec, w_ada, b_ada):
    out = pl.pallas_call(
        _ada_kernel,
        grid=(DEPTH, 9 * D_MODEL // ADA_COLS),
        in_specs=[
            pl.BlockSpec((SUBLANES, D_MODEL), lambda i, j: (0, 0)),
            pl.BlockSpec((None, D_MODEL, ADA_COLS), lambda i, j: (i, 0, j)),
            pl.BlockSpec((None, 1, ADA_COLS), lambda i, j: (i, 0, j)),
        ],
        out_specs=pl.BlockSpec((None, SUBLANES, ADA_COLS), lambda i, j: (i, 0, j)),
        out_shape=jax.ShapeDtypeStruct((DEPTH, SUBLANES, 9 * D_MODEL), F32),
        compiler_params=_cparams(("arbitrary", "arbitrary")),
        name="ada_mods",
    )(cvec, w_ada, b_ada.reshape(DEPTH, 1, 9 * D_MODEL))
    mods = out.reshape(DEPTH, SUBLANES, 9, D_MODEL)[:, :BATCH + 1]
    return jnp.pad(mods, ((0, 0), (0, 0), (0, MOD_ROWS - 9), (0, 0)))


def _gelu_tanh(g):
    return 0.5 * g * (1.0 + jnp.tanh(np.sqrt(2.0 / np.pi).astype(np.float32) * (g + 0.044715 * (g * g * g))))


def _ffn_kernel(*refs, sub, mixer, final):
    x_ref, mod_ref, ng_ref, wgu_ref, wd_ref = refs[:5]
    rest = list(refs[5:])
    if mixer == "attn":
        a_ref, wo_ref = rest[:2]
        rest = rest[2:]
        mixed = a_ref[...]
    elif mixer == "lru":
        hf_ref, hbk_ref, g_ref, wo_ref = rest[:4]
        rest = rest[4:]
        h = jnp.concatenate(
            [jnp.concatenate([_from_scan_order(hf_ref[n, ch * LRU_T:(ch + 1) * LRU_T, :]
                                               + hbk_ref[n, ch * LRU_T:(ch + 1) * LRU_T, :])
                              for ch in range(CHUNKS_PER_TILE)], axis=0)
             for n in range(LRU_BLOCKS)], axis=1)
        mixed = (h * _gelu_tanh(g_ref[...])).astype(BF16)
    elif mixer == "join":
        ctx_ref = rest.pop(0)
    fg_ref = rest.pop(0) if final else None
    o_ref, hb_ref, acc_ref, xm_ref = rest
    if mixer is None:
        xm_ref = x_ref
    elif mixer == "join":
        xm_ref[...] = jnp.where(pl.program_id(0) < X_TILES, x_ref[...], ctx_ref[...])
    else:
        y = jnp.dot(mixed, wo_ref[...], preferred_element_type=F32)
        xm_ref[...] = x_ref[...] + mod_ref[5:6, :] * y
    hb_ref[...] = _adaln(xm_ref[...], mod_ref, ng_ref, sub).astype(BF16)
    acc_ref[...] = jnp.zeros_like(acc_ref)

    for c in range(N_FF_CHUNKS):
        lo = c * FF_CHUNK
        g = jnp.dot(hb_ref[...], wgu_ref[:, lo:lo + FF_CHUNK], preferred_element_type=F32)
        u = jnp.dot(hb_ref[...], wgu_ref[:, D_FF + lo:D_FF + lo + FF_CHUNK], preferred_element_type=F32)
        a = ((g * jax.nn.sigmoid(g)) * u).astype(BF16)
        acc_ref[...] += jnp.dot(a, wd_ref[lo:lo + FF_CHUNK, :], preferred_element_type=F32)
    gate = mod_ref[3 * sub + 2:3 * sub + 3, :]
    y = xm_ref[...] + (FFN_RES * gate) * acc_ref[...]
    if final:
        y = (y * lax.rsqrt(jnp.mean(y * y, axis=-1, keepdims=True) + EPS)) * fg_ref[...]
    o_ref[...] = y


def _ffn(xs, mods_i, ng_i, wgu, wd, layer, which, sub, ctx=None, attn=None, lru=None, w_o=None, final_g=None):
    final = final_g is not None
    n_tiles = X_TILES if final else N_TILES
    w_map = lambda i: (layer, which, 0, 0)
    row_spec = lambda cols: pl.BlockSpec((TM, cols), lambda i: (i, 0))
    x_spec = row_spec(D_MODEL)
    if ctx is not None:
        x_spec = pl.BlockSpec((TM, D_MODEL), lambda i: (jnp.minimum(i, X_TILES - 1), 0))
    in_specs = [
        x_spec,
        pl.BlockSpec((None, MOD_ROWS, D_MODEL), lambda i: (_tile_kind(i), 0, 0)),
        _const_spec((SUBLANES, D_MODEL)),
        pl.BlockSpec((None, None, D_MODEL, 2 * D_FF), w_map, pipeline_mode=pl.Buffered(1)),
        pl.BlockSpec((None, None, D_FF, D_MODEL), w_map, pipeline_mode=pl.Buffered(1)),
    ]
    args = [xs, mods_i, ng_i, wgu, wd]
    mixer = None
    if attn is not None:
        mixer = "attn"
        in_specs += [row_spec(D_MODEL), _const_spec((D_MODEL, D_MODEL))]
        args += [attn, w_o]
    elif lru is not None:
        mixer = "lru"
        h_spec = pl.BlockSpec((LRU_BLOCKS, TM, LRU_BW), lambda i: (0, i, 0))
        in_specs += [h_spec, h_spec, row_spec(D_RNN), _const_spec((D_RNN, D_MODEL))]
        args += [*lru, w_o]
    elif ctx is not None:
        mixer = "join"
        in_specs.append(_const_spec((N_CTX, D_MODEL)))
        args.append(ctx)
    if final:
        in_specs.append(_const_spec((1, D_MODEL)))
        args.append(final_g.reshape(1, D_MODEL))
    scratch = [pltpu.VMEM((TM, D_MODEL), BF16), pltpu.VMEM((TM, D_MODEL), F32), pltpu.VMEM((TM, D_MODEL), F32)]
    return pl.pallas_call(
        functools.partial(_ffn_kernel, sub=sub, mixer=mixer, final=final),
        grid=(n_tiles,),
        in_specs=in_specs,
        out_specs=row_spec(D_MODEL),
        out_shape=jax.ShapeDtypeStruct((n_tiles * TM, D_MODEL), F32),
        scratch_shapes=scratch,
        compiler_params=_cparams(("arbitrary",)),
        name="ffn_" + (mixer or "plain") + ("_final" if final else ""),
    )(*args)


NMM_COLS = 512


def _rope_tile(y, c, s1, s2):
    return y * c + pltpu.roll(y, LANES - 16, 1) * s1 + pltpu.roll(y, 16, 1) * s2


def _nmm_kernel(*refs, n_cols, rope_cols, out_dtype):
    if rope_cols:
        x_ref, mod_ref, ng_ref, w_ref, c_ref, s1_ref, s2_ref, o_ref = refs
    else:
        x_ref, mod_ref, ng_ref, w_ref, o_ref = refs
    hb = _adaln(x_ref[...], mod_ref, ng_ref, 1).astype(BF16)
    for c0 in range(0, n_cols, NMM_COLS):
        y = jnp.dot(hb, w_ref[:, c0:c0 + NMM_COLS], preferred_element_type=F32)
        if c0 < rope_cols:
            c, s1, s2 = c_ref[...], s1_ref[...], s2_ref[...]
            y = jnp.concatenate(
                [_rope_tile(y[:, t:t + LANES], c, s1, s2) for t in range(0, NMM_COLS, LANES)], axis=1)
        o_ref[:, c0:c0 + NMM_COLS] = y.astype(out_dtype)


def _nmm(xs, mods_i, ng_i, w, out_dtype, rope=None, rope_cols=0):
    n_cols = w.shape[1]
    in_specs = [
        pl.BlockSpec((TM, D_MODEL), lambda i: (i, 0)),
        pl.BlockSpec((None, MOD_ROWS, D_MODEL), lambda i: (_tile_kind(i), 0, 0)),
        _const_spec((SUBLANES, D_MODEL)),
        _const_spec((D_MODEL, n_cols)),
    ]
    args = [xs, mods_i, ng_i, w]
    if rope_cols:
        rmap = lambda i: (jnp.where(i < X_TILES, i % TILES_PER_BATCH, TILES_PER_BATCH), 0)
        in_specs += [pl.BlockSpec((TM, LANES), rmap)] * 3
        args += list(rope)
    return pl.pallas_call(
        functools.partial(_nmm_kernel, n_cols=n_cols, rope_cols=rope_cols, out_dtype=out_dtype),
        grid=(N_TILES,),
        in_specs=in_specs,
        out_specs=pl.BlockSpec((TM, n_cols), lambda i: (i, 0)),
        out_shape=jax.ShapeDtypeStruct((N_ALL, n_cols), out_dtype),
        compiler_params=_cparams(("arbitrary",)),
        name="adaln_proj",
    )(*args)


def _rope_tables():
    quarter = HEAD_DIM // 4
    pos = jnp.arange(SEQ)
    inv = ROPE_BASE ** (-jnp.arange(quarter, dtype=F32) / quarter)
    ang_row = (pos // GRID_W).astype(F32)[:, None] * inv
    ang_col = (pos % GRID_W).astype(F32)[:, None] * inv
    cr, sr, cc, sc = jnp.cos(ang_row), jnp.sin(ang_row), jnp.cos(ang_col), jnp.sin(ang_col)
    zero = jnp.zeros_like(sr)
    heads = LANES // HEAD_DIM
    cos = jnp.concatenate([cr, cr, cc, cc] * heads, axis=1)
    s1 = jnp.concatenate([-sr, zero, -sc, zero] * heads, axis=1)
    s2 = jnp.concatenate([zero, sr, zero, sc] * heads, axis=1)
    ident = jnp.ones((TM, LANES), F32)
    zero = jnp.zeros((TM, LANES), F32)
    return (jnp.concatenate([cos, ident]), jnp.concatenate([s1, zero]), jnp.concatenate([s2, zero]))


LRU_T = 256
LRU_SEG = LRU_T // SUBLANES
X_CHUNKS = SEQ // LRU_T
HALO = SUBLANES
CHUNKS_PER_TILE = TM // LRU_T
PROJ_GROUP = 256


def _to_scan_order(v):
    return jnp.swapaxes(v.reshape(SUBLANES, LRU_SEG, LRU_BW), 0, 1).reshape(LRU_T, LRU_BW)


def _from_scan_order(v):
    return jnp.swapaxes(v.reshape(LRU_SEG, SUBLANES, LRU_BW), 0, 1).reshape(LRU_T, LRU_BW)


def _lru_in_kernel(x_ref, xp_ref, xn_ref, mod_ref, ng_ref, w_ref, cw_ref, g_ref, xc_ref, hb_ref):
    i = pl.program_id(0)
    is_ctx = i >= X_TILES
    first = i % TILES_PER_BATCH == 0
    last = i % TILES_PER_BATCH == TILES_PER_BATCH - 1
    hb_ref[0:TM, :] = _adaln(x_ref[...], mod_ref, ng_ref, 1).astype(BF16)
    halo = jnp.concatenate([xp_ref[...], xn_ref[...]], axis=0)
    hb_ref[TM:, :] = _adaln(halo, mod_ref, ng_ref, 1).astype(BF16)
    sub = lax.broadcasted_iota(jnp.int32, (SUBLANES, LRU_BW), 0)
    pad_before = [jnp.logical_or(is_ctx, first) if ch == 0 else is_ctx for ch in range(CHUNKS_PER_TILE)]
    pad_after = [jnp.logical_or(is_ctx, last) if ch == CHUNKS_PER_TILE - 1 else is_ctx
                 for ch in range(CHUNKS_PER_TILE)]

    def conv(n, ch, chunk, before, after):
        cols = slice(n * LRU_BW, (n + 1) * LRU_BW)
        cb = cw_ref[CONV_W:CONV_W + 1, cols]
        taps = [cw_ref[j:j + 1, cols] for j in range(CONV_W)]
        before = jnp.where(pad_before[ch], 0.0, before)
        after = jnp.where(pad_after[ch], 0.0, after)
        xp = _to_scan_order(chunk)
        xg = [xp[k * SUBLANES:(k + 1) * SUBLANES] for k in range(LRU_SEG)]
        xm1 = jnp.where(sub == 0, before, pltpu.roll(xg[LRU_SEG - 1], 1, 0))
        xp0 = jnp.where(sub == SUBLANES - 1, after[0:1], pltpu.roll(xg[0], SUBLANES - 1, 0))
        xp1 = jnp.where(sub == SUBLANES - 1, after[1:2], pltpu.roll(xg[1], SUBLANES - 1, 0))
        xe = [xm1] + xg + [xp0, xp1]
        for k in range(LRU_SEG):
            xc = cb + xe[k] * taps[0] + xe[k + 1] * taps[1] + xe[k + 2] * taps[2] + xe[k + 3] * taps[3]
            xc_ref[n, ch * LRU_T + k * SUBLANES:ch * LRU_T + (k + 1) * SUBLANES, :] = xc

    for c0 in range(0, D_RNN, PROJ_GROUP):
        wx = w_ref[:, D_RNN + c0:D_RNN + c0 + PROJ_GROUP]
        y = jnp.dot(hb_ref[...], wx, preferred_element_type=F32)
        g_ref[:, c0:c0 + PROJ_GROUP] = jnp.dot(hb_ref[0:TM, :], w_ref[:, c0:c0 + PROJ_GROUP],
                                               preferred_element_type=F32)
        for t in range(PROJ_GROUP // LRU_BW):
            lanes = slice(t * LRU_BW, (t + 1) * LRU_BW)
            ext = jnp.concatenate([y[TM:TM + HALO, lanes], y[0:TM, lanes], y[TM + HALO:, lanes]], axis=0)
            for ch in range(CHUNKS_PER_TILE):
                lo = HALO + ch * LRU_T
                conv(c0 // LRU_BW + t, ch, ext[lo:lo + LRU_T], ext[lo - 1:lo], ext[lo + LRU_T:lo + LRU_T + 2])


def _lru_in_proj(xs, mods_i, ng_i, w, cw):
    blocks_per_tile = TM // HALO
    n_halo_blocks = N_ALL // HALO
    return pl.pallas_call(
        _lru_in_kernel,
        grid=(N_TILES,),
        in_specs=[
            pl.BlockSpec((TM, D_MODEL), lambda i: (i, 0)),
            pl.BlockSpec((HALO, D_MODEL), lambda i: (jnp.maximum(i * blocks_per_tile - 1, 0), 0)),
            pl.BlockSpec((HALO, D_MODEL), lambda i: (jnp.minimum((i + 1) * blocks_per_tile, n_halo_blocks - 1), 0)),
            pl.BlockSpec((None, MOD_ROWS, D_MODEL), lambda i: (_tile_kind(i), 0, 0)),
            _const_spec((SUBLANES, D_MODEL)),
            _const_spec((D_MODEL, 2 * D_RNN)),
            _const_spec((SUBLANES, D_RNN)),
        ],
        out_specs=[pl.BlockSpec((TM, D_RNN), lambda i: (i, 0)),
                   pl.BlockSpec((LRU_BLOCKS, TM, LRU_BW), lambda i: (0, i, 0))],
        out_shape=[jax.ShapeDtypeStruct((N_ALL, D_RNN), F32),
                   jax.ShapeDtypeStruct((LRU_BLOCKS, N_ALL, LRU_BW), F32)],
        scratch_shapes=[pltpu.VMEM((TM + 2 * HALO, D_MODEL), BF16)],
        compiler_params=_cparams(("arbitrary",)),
        name="lru_in_proj",
    )(xs, xs, xs, mods_i, ng_i, w, cw)


def _softplus(x):
    return jnp.maximum(x, 0.0) + jnp.log1p(jnp.exp(-jnp.abs(x)))


def _lru_kernel(xf_ref, xb_ref, gwf_ref, gwb_ref, gbf_ref, gbb_ref, of_ref, ob_ref,
                h_ref, p_ref, carry_ref):
    @pl.when(pl.program_id(1) == 0)
    def _():
        carry_ref[...] = jnp.zeros_like(carry_ref)

    for n in range(LRU_BLOCKS):
        _lru_block(n, xf_ref, gwf_ref, gbf_ref, of_ref, h_ref.at[0], p_ref.at[0], carry_ref.at[0], False)
        _lru_block(n, xb_ref, gwb_ref, gbb_ref, ob_ref, h_ref.at[1], p_ref.at[1], carry_ref.at[1], True)


def _lru_block(n, xc_ref, gw_ref, gb_ref, o_ref, h_ref, p_ref, carry_ref, reverse):
    sub = lax.broadcasted_iota(jnp.int32, (SUBLANES, LRU_BW), 0)
    steps = range(LRU_SEG - 1, -1, -1) if reverse else range(LRU_SEG)
    segs = range(SUBLANES - 1, -1, -1) if reverse else range(SUBLANES)

    cols = slice(n * LRU_BW, (n + 1) * LRU_BW)
    xc = xc_ref[n]
    ri = jnp.dot(xc.astype(BF16), gw_ref[n], preferred_element_type=F32)
    r2 = jnp.tanh(ri[:, :LRU_BW] + gb_ref[0:1, cols]) + 1.0
    i2 = jnp.tanh(ri[:, LRU_BW:] + gb_ref[1:2, cols]) + 1.0
    log_a = r2 * ((-0.5 * LRU_C) * _softplus(-gb_ref[2:3, cols]))
    a = jnp.exp(log_a)
    t = jnp.tanh(log_a)
    z = (-0.5 * t) / (1.0 - t)
    b = jnp.where(z > 0.0, z * lax.rsqrt(z), 0.0) * (i2 * xc)

    h = jnp.zeros((SUBLANES, LRU_BW), F32)
    p = jnp.ones((SUBLANES, LRU_BW), F32)
    for k in steps:
        grp = slice(k * SUBLANES, (k + 1) * SUBLANES)
        h = a[grp] * h + b[grp]
        p = a[grp] * p
        h_ref[n, grp, :] = h
        p_ref[n, grp, :] = p
    st = carry_ref[n, 0:1, :]
    enter = jnp.zeros((SUBLANES, LRU_BW), F32)
    for s in segs:
        enter = jnp.where(sub == s, st, enter)
        st = p[s:s + 1, :] * st + h[s:s + 1, :]
    carry_ref[n, 0:1, :] = st
    for k in range(LRU_SEG):
        grp = slice(k * SUBLANES, (k + 1) * SUBLANES)
        o_ref[n, grp, :] = h_ref[n, grp, :] + p_ref[n, grp, :] * enter


def _lru_scan(xc, gw, gb):
    def row_block(reverse):
        def f(b, m):
            c = (X_CHUNKS - m) if reverse else (m - 1)
            return (0, jnp.where(m == 0, N_X // LRU_T + b, b * X_CHUNKS + c), 0)
        return f

    chunk = lambda reverse: pl.BlockSpec((LRU_BLOCKS, LRU_T, LRU_BW), row_block(reverse))
    gw_spec = _const_spec((LRU_BLOCKS, LRU_BW, 2 * LRU_BW))
    gb_spec = _const_spec((SUBLANES, D_RNN))
    out = jax.ShapeDtypeStruct((LRU_BLOCKS, N_ALL, LRU_BW), F32)
    return pl.pallas_call(
        _lru_kernel,
        grid=(BATCH, X_CHUNKS + 1),
        in_specs=[chunk(False), chunk(True), gw_spec, gw_spec, gb_spec, gb_spec],
        out_specs=[chunk(False), chunk(True)],
        out_shape=[out, out],
        scratch_shapes=[
            pltpu.VMEM((2, LRU_BLOCKS, LRU_T, LRU_BW), F32),
            pltpu.VMEM((2, LRU_BLOCKS, LRU_T, LRU_BW), F32),
            pltpu.VMEM((2, LRU_BLOCKS, SUBLANES, LRU_BW), F32),
        ],
        compiler_params=_cparams(("arbitrary", "arbitrary")),
        name="lru_scan",
    )(xc, xc, gw[0], gw[1], gb[0], gb[1])


SWA_TQ = 128
SWA_XQ = SEQ // SWA_TQ
SWA_QB = CTX_LEN // SWA_TQ
SWA_STEP = SWA_QB * SWA_TQ
SWA_XSTEPS = SEQ // SWA_STEP
SWA_SPAN = 3 * SWA_TQ
KV_COLS = 2 * B_KV_HEADS * LANES
_NT = (((1,), (1,)), ((), ()))


def _split_heads(qt):
    lo = lax.broadcasted_iota(jnp.int32, qt.shape, 1) < HEAD_DIM
    zero = jnp.zeros_like(qt)
    return jnp.concatenate([jnp.where(lo, qt, zero), jnp.where(lo, zero, qt)], axis=0)


def _merge_heads(o):
    t = o.shape[0] // 2
    lo = lax.broadcasted_iota(jnp.int32, (t, LANES), 1) < HEAD_DIM
    return jnp.where(lo, o[:t], o[t:])


def _swa_kernel(sink_ref, q_ref, kvp_ref, kvc_ref, kvn_ref, kvx_ref, o_ref):
    j = pl.program_id(1)
    is_x = j < SWA_XSTEPS
    iq = lax.broadcasted_iota(jnp.int32, (SWA_TQ, SWA_SPAN), 0)
    ik = lax.broadcasted_iota(jnp.int32, (SWA_TQ, SWA_SPAN), 1)
    band = jnp.where(jnp.abs(iq + WINDOW - ik) <= WINDOW, 0.0, NEG)
    biases = []
    for u in range(SWA_QB):
        jb = j * SWA_QB + u
        lo_ok = jnp.where(jb > 0, 0, SWA_TQ)
        hi_ok = jnp.where(jb < SWA_XQ - 1, SWA_SPAN, 2 * SWA_TQ)
        hi_ok = jnp.where(is_x, hi_ok, 0)
        b1 = jnp.where(ik >= lo_ok, band, NEG)
        b1 = jnp.where(ik < hi_ok, b1, NEG)
        biases.append(jnp.concatenate([b1, b1], axis=0))
    top = lax.broadcasted_iota(jnp.int32, (2 * SWA_TQ, 1), 0) < SWA_TQ
    for g in range(B_KV_HEADS):
        kcol = slice(g * LANES, (g + 1) * LANES)
        vcol = slice((B_KV_HEADS + g) * LANES, (B_KV_HEADS + g + 1) * LANES)
        own = [slice(u * SWA_TQ, (u + 1) * SWA_TQ) for u in range(SWA_QB)]
        kb = [kvp_ref[:, kcol]] + [kvc_ref[r, kcol] for r in own] + [kvn_ref[:, kcol]]
        vb = [kvp_ref[:, vcol]] + [kvc_ref[r, vcol] for r in own] + [kvn_ref[:, vcol]]
        kx = kvx_ref[:, kcol]
        vx = kvx_ref[:, vcol]
        for u in range(SWA_QB):
            kd = jnp.concatenate(kb[u:u + 3], axis=0)
            vd = jnp.concatenate(vb[u:u + 3], axis=0)
            for t in range(2):
                tile = 2 * g + t
                cols = slice(tile * LANES, (tile + 1) * LANES)
                qs = _split_heads(q_ref[own[u], cols])
                s_loc = lax.dot_general(qs, kd, _NT, preferred_element_type=F32) + biases[u]
                s_ctx = lax.dot_general(qs, kx, _NT, preferred_element_type=F32)
                sink = jnp.where(top, sink_ref[2 * tile], sink_ref[2 * tile + 1])
                mx = jnp.maximum(jnp.maximum(jnp.max(s_loc, axis=-1, keepdims=True),
                                             jnp.max(s_ctx, axis=-1, keepdims=True)), sink)
                p_loc = jnp.exp(s_loc - mx)
                p_ctx = jnp.exp(s_ctx - mx)
                den = (jnp.sum(p_loc, axis=-1, keepdims=True) + jnp.sum(p_ctx, axis=-1, keepdims=True)
                       + jnp.exp(sink - mx))
                o = (jnp.dot(p_loc.astype(BF16), vd, preferred_element_type=F32)
                     + jnp.dot(p_ctx.astype(BF16), vx, preferred_element_type=F32)) / den
                o_ref[own[u], cols] = _merge_heads(o).astype(BF16)


def _swa_attention(qkv, sinks):
    def q_block(b, j):
        return jnp.where(j < SWA_XSTEPS, b * SWA_XSTEPS + j, N_X // SWA_STEP + b)

    def edge_block(off):
        return lambda b, j: (b * SWA_XQ + jnp.clip(j * SWA_QB + off, 0, SWA_XQ - 1), 1)

    return pl.pallas_call(
        _swa_kernel,
        grid=(BATCH, SWA_XSTEPS + 1),
        in_specs=[
            pl.BlockSpec(memory_space=pltpu.SMEM),
            pl.BlockSpec((SWA_STEP, D_MODEL), lambda b, j: (q_block(b, j), 0)),
            pl.BlockSpec((SWA_TQ, KV_COLS), edge_block(-1)),
            pl.BlockSpec((SWA_STEP, KV_COLS), lambda b, j: (b * SWA_XSTEPS + jnp.minimum(j, SWA_XSTEPS - 1), 1)),
            pl.BlockSpec((SWA_TQ, KV_COLS), edge_block(SWA_QB)),
            pl.BlockSpec((CTX_LEN, KV_COLS), lambda b, j: (N_X // CTX_LEN + b, 1)),
        ],
        out_specs=pl.BlockSpec((SWA_STEP, D_MODEL), lambda b, j: (q_block(b, j), 0)),
        out_shape=jax.ShapeDtypeStruct((N_ALL, D_MODEL), BF16),
        compiler_params=_cparams(("arbitrary", "arbitrary")),
        name="swa_attn",
    )(sinks, qkv, qkv, qkv, qkv, qkv)


def _prep_swa_weights(w_qkv):
    nq = B_HEADS * HEAD_DIM
    nk = B_KV_HEADS * HEAD_DIM
    wq = w_qkv[:, :nq] * (HEAD_DIM ** -0.5)
    dup = lambda w: jnp.tile(w.reshape(D_MODEL, B_KV_HEADS, 1, HEAD_DIM), (1, 1, 2, 1)).reshape(D_MODEL, -1)
    wk = dup(w_qkv[:, nq:nq + nk])
    wv = dup(w_qkv[:, nq + nk:])
    return jnp.concatenate([wq, wk, wv], axis=1).astype(BF16)


NAT_ROWS = 4
NAT_TQ = NAT_ROWS * GRID_W
NAT_XQ = SEQ // NAT_TQ
NAT_SPAN = 3 * NAT_TQ
N_HEAD_TILES = C_HEADS * HEAD_DIM // LANES
GRID_ROWS = SEQ // GRID_W


NAT_KEY_ROWS = NAT_SPAN // GRID_W
NAT_RPB_ROWS = 2 * WIN_H
NAT_PATTERNS = 4


def _nat_pattern(j):
    return jnp.where(j == 0, 0, jnp.where(j == NAT_XQ - 1, 2, jnp.where(j == NAT_XQ, 3, 1)))


def _nat_kernel(sel_ref, q_ref, kp_ref, kc_ref, kn_ref, vp_ref, vc_ref, vn_ref, kx_ref, vx_ref, tab_ref,
                o_ref, bias_ref):
    j = pl.program_id(1)

    @pl.when(jnp.logical_or(j <= 1, j >= NAT_XQ - 1))
    def _():
        lo = lax.broadcasted_iota(jnp.int32, (GRID_W, LANES), 1) < GRID_W
        base = _nat_pattern(j) * (NAT_ROWS * NAT_KEY_ROWS)
        for qr in range(NAT_ROWS):
            rows = slice(qr * GRID_W, (qr + 1) * GRID_W)
            for kp in range(NAT_KEY_ROWS // 2):
                e0 = sel_ref[base + qr * NAT_KEY_ROWS + 2 * kp]
                e1 = sel_ref[base + qr * NAT_KEY_ROWS + 2 * kp + 1]

                def fill(h, carry, rows=rows, kp=kp, e0=e0, e1=e1):
                    bias_ref[h, rows, kp * LANES:(kp + 1) * LANES] = jnp.where(lo, tab_ref[h, e0], tab_ref[h, e1])
                    return carry

                lax.fori_loop(0, C_HEADS, fill, 0)

    for t in range(N_HEAD_TILES):
        cols = slice(t * LANES, (t + 1) * LANES)
        qs = _split_heads(q_ref[:, cols])
        kd = jnp.concatenate([kp_ref[:, cols], kc_ref[:, cols], kn_ref[:, cols]], axis=0)
        vd = jnp.concatenate([vp_ref[:, cols], vc_ref[:, cols], vn_ref[:, cols]], axis=0)
        bias = bias_ref[2 * t:2 * t + 2].reshape(2 * NAT_TQ, NAT_SPAN)
        s_loc = lax.dot_general(qs, kd, _NT, preferred_element_type=F32) + bias
        s_ctx = lax.dot_general(qs, kx_ref[:, cols], _NT, preferred_element_type=F32)
        mx = jnp.maximum(jnp.max(s_loc, axis=-1, keepdims=True), jnp.max(s_ctx, axis=-1, keepdims=True))
        p_loc = jnp.exp(s_loc - mx)
        p_ctx = jnp.exp(s_ctx - mx)
        den = jnp.sum(p_loc, axis=-1, keepdims=True) + jnp.sum(p_ctx, axis=-1, keepdims=True)
        o = (jnp.dot(p_loc.astype(BF16), vd, preferred_element_type=F32)
             + jnp.dot(p_ctx.astype(BF16), vx_ref[:, cols], preferred_element_type=F32)) / den
        o_ref[:, cols] = _merge_heads(o).astype(BF16)


def _nat_attention(qkv, row_sel, tab):
    nq = NAT_XQ + 1

    def q_block(b, j):
        return jnp.where(j < NAT_XQ, b * NAT_XQ + j, N_X // NAT_TQ + b)

    def kv_spec(off, part):
        return pl.BlockSpec(
            (NAT_TQ, D_MODEL), lambda b, j: (b * NAT_XQ + jnp.clip(j + off, 0, NAT_XQ - 1), part))

    def ctx_spec(part):
        return pl.BlockSpec((CTX_LEN, D_MODEL), lambda b, j: (N_X // CTX_LEN + b, part))

    return pl.pallas_call(
        _nat_kernel,
        grid=(BATCH, nq),
        in_specs=[
            pl.BlockSpec(memory_space=pltpu.SMEM),
            pl.BlockSpec((NAT_TQ, D_MODEL), lambda b, j: (q_block(b, j), 0)),
            kv_spec(-1, 1), kv_spec(0, 1), kv_spec(1, 1),
            kv_spec(-1, 2), kv_spec(0, 2), kv_spec(1, 2),
            ctx_spec(1), ctx_spec(2),
            _const_spec((C_HEADS, NAT_RPB_ROWS, GRID_W, LANES)),
        ],
        out_specs=pl.BlockSpec((NAT_TQ, D_MODEL), lambda b, j: (q_block(b, j), 0)),
        out_shape=jax.ShapeDtypeStruct((N_ALL, D_MODEL), BF16),
        scratch_shapes=[pltpu.VMEM((C_HEADS, NAT_TQ, NAT_SPAN), F32)],
        compiler_params=_cparams(("arbitrary", "arbitrary")),
        name="nat_attn",
    )(row_sel, qkv, qkv, qkv, qkv, qkv, qkv, qkv, qkv, qkv, tab)


def _nat_row_select():
    qr = np.arange(NAT_ROWS)
    kr = np.arange(NAT_KEY_ROWS) - NAT_ROWS
    row_sel = []
    for r0 in (0, NAT_ROWS, GRID_ROWS - NAT_ROWS):
        r = r0 + qr
        rs = np.clip(r - WIN_H // 2, 0, GRID_ROWS - WIN_H)
        k_abs = r0 + kr
        row_ok = (k_abs[None, :] >= rs[:, None]) & (k_abs[None, :] < rs[:, None] + WIN_H)
        row_sel.append(np.where(row_ok, k_abs[None, :] - r[:, None] + (WIN_H - 1), NAT_RPB_ROWS - 1))
    row_sel.append(np.full((NAT_ROWS, NAT_KEY_ROWS), NAT_RPB_ROWS - 1))
    return jnp.asarray(np.stack(row_sel).reshape(-1), jnp.int32)


def _nat_bias_tiles(rpb):
    col = np.arange(GRID_W)
    col_start = np.clip(col - WIN_W // 2, 0, GRID_W - WIN_W)
    col_ok = (col[None, :] >= col_start[:, None]) & (col[None, :] < col_start[:, None] + WIN_W)
    col_off = np.clip(col[None, :] - col[:, None] + (WIN_W - 1), 0, 2 * WIN_W - 2)
    by_col = jnp.where(col_ok[None, None], rpb[:, :, col_off], NEG)
    by_col = jnp.concatenate([by_col, jnp.full((C_HEADS, 1, GRID_W, GRID_W), NEG, F32)], axis=1)
    return jnp.concatenate([by_col, by_col], axis=-1)


def _prep_nat_weights(w_qkv):
    nq = C_HEADS * HEAD_DIM
    return jnp.concatenate([w_qkv[:, :nq] * (HEAD_DIM ** -0.5), w_qkv[:, nq:]], axis=1).astype(BF16)


def _pad_rows(a, rows=SUBLANES):
    return jnp.pad(a, ((0, rows - a.shape[0]), (0, 0)))


def kernel(x, c, ctx, c_ctx, w_ada, b_ada, norm_g, w_ffn_gu, w_ffn_down, a_w_in, a_conv_w, a_conv_b,
           a_gate_w, a_gate_b, a_lambda, a_w_out, b_w_qkv, b_sinks, b_w_o, c_w_qkv, c_rpb, c_w_o, final_g):
    xs = x.reshape(N_X, D_MODEL)
    cvec = _pad_rows(jnp.concatenate([c, c_ctx[None, :]], axis=0))
    mods = _ada_mods(cvec, w_ada, b_ada)
    rope = _rope_tables()
    wgu, wd = w_ffn_gu.astype(BF16), w_ffn_down.astype(BF16)
    for i in range(DEPTH):
        kind, j = i % N_MIXERS, i // N_MIXERS
        last = i == DEPTH - 1
        ng = _pad_rows(norm_g[i])
        xs = _ffn(xs, mods[i], ng, wgu, wd, i, 0, 0, ctx=ctx.reshape(N_CTX, D_MODEL) if i == 0 else None)
        if kind == 0:
            cw = _pad_rows(jnp.concatenate([a_conv_w[j], a_conv_b[j][None, :]], axis=0))
            g_branch, xc = _lru_in_proj(xs, mods[i], ng, a_w_in[j].astype(BF16), cw)
            gb = [_pad_rows(jnp.concatenate([0.5 * a_gate_b[j, d], a_lambda[j, d][None, :]], axis=0))
                  for d in range(2)]
            gw = [(0.5 * jnp.concatenate([a_gate_w[j, d, 0], a_gate_w[j, d, 1]], axis=-1)).astype(BF16)
                  for d in range(2)]
            hf, hb = _lru_scan(xc, gw, gb)
            mixed = dict(lru=(hf, hb, g_branch), w_o=a_w_out[j].astype(BF16))
        elif kind == 1:
            qkv = _nmm(xs, mods[i], ng, _prep_swa_weights(b_w_qkv[j]), BF16, rope=rope,
                       rope_cols=B_HEADS * HEAD_DIM + B_KV_HEADS * LANES)
            mixed = dict(attn=_swa_attention(qkv, b_sinks[j]), w_o=b_w_o[j].astype(BF16))
        else:
            qkv = _nmm(xs, mods[i], ng, _prep_nat_weights(c_w_qkv[j]), BF16)
            o = _nat_attention(qkv, _nat_row_select(), _nat_bias_tiles(c_rpb[j]))
            mixed = dict(attn=o, w_o=c_w_o[j].astype(BF16))
        xs = _ffn(xs, mods[i], ng, wgu, wd, i, 1, 2, final_g=final_g if last else None, **mixed)
    return xs.reshape(BATCH, SEQ, D_MODEL)
```

```python
import functools

import numpy as np
import jax
import jax.numpy as jnp
from jax import lax
from jax.experimental import pallas as pl
from jax.experimental.pallas import tpu as pltpu

D_MODEL = 1024
BATCH = 2
SEQ = 16384
DEPTH = 4
GRID_W = 64
CTX_LEN = 256
N_MIXERS = 3
EPS = 1e-6
FFN_RES = 0.5
D_FF = 2816
D_RNN = 1536
LRU_BLOCKS = 12
LRU_BW = D_RNN // LRU_BLOCKS
CONV_W = 4
LRU_C = 8.0
HEAD_DIM = 64
B_HEADS = 16
B_KV_HEADS = 4
WINDOW = 128
ROPE_BASE = 10000.0
C_HEADS = 16
WIN_H = 8
WIN_W = 16
NEG = -1e30

LANES = 128
SUBLANES = 8
VMEM_LIMIT = 56 * 1024 * 1024

N_X = BATCH * SEQ
N_CTX = BATCH * CTX_LEN
N_ALL = N_X + N_CTX
TM = 512
N_TILES = N_ALL // TM
X_TILES = N_X // TM
TILES_PER_BATCH = SEQ // TM
MOD_ROWS = 16
FF_CHUNK = 256
N_FF_CHUNKS = D_FF // FF_CHUNK

F32 = jnp.float32
BF16 = jnp.bfloat16


def _cparams(sem):
    return pltpu.CompilerParams(dimension_semantics=sem, vmem_limit_bytes=VMEM_LIMIT)


def _const_spec(shape):
    nd = len(shape)
    return pl.BlockSpec(shape, lambda *_: (0,) * nd, pipeline_mode=pl.Buffered(1))


def _tile_kind(i):
    return jnp.minimum(i // TILES_PER_BATCH, BATCH)


def _adaln(x, mod_ref, ng_ref, sub):
    g = ng_ref[sub:sub + 1, :]
    shift = mod_ref[3 * sub:3 * sub + 1, :]
    scale = mod_ref[3 * sub + 1:3 * sub + 2, :]
    y = x * lax.rsqrt(jnp.mean(x * x, axis=-1, keepdims=True) + EPS)
    return (y * g) * (1.0 + scale) + shift


ADA_COLS = 1152


def _ada_kernel(c_ref, w_ref, b_ref, o_ref):
    c = c_ref[...]
    sc = c * jax.nn.sigmoid(c)
    o_ref[...] = jnp.dot(sc, w_ref[...], preferred_element_type=F32,
                         precision=lax.Precision.HIGHEST) + b_ref[...]


def _ada_mods(cvec, w_ada, b_ada):
    out = pl.pallas_call(
        _ada_kernel,
        grid=(DEPTH, 9 * D_MODEL // ADA_COLS),
        in_specs=[
            pl.BlockSpec((SUBLANES, D_MODEL), lambda i, j: (0, 0)),
            pl.BlockSpec((None, D_MODEL, ADA_COLS), lambda i, j: (i, 0, j)),
            pl.BlockSpec((None, 1, ADA_COLS), lambda i, j: (i, 0, j)),
        ],
        out_specs=pl.BlockSpec((None, SUBLANES, ADA_COLS), lambda i, j: (i, 0, j)),
        out_shape=jax.ShapeDtypeStruct((DEPTH, SUBLANES, 9 * D_MODEL), F32),
        compiler_params=_cparams(("arbitrary", "arbitrary")),
        name="ada_mods",
    )(cvec, w_ada, b_ada.reshape(DEPTH, 1, 9 * D_MODEL))
    mods = out.reshape(DEPTH, SUBLANES, 9, D_MODEL)[:, :BATCH + 1]
    return jnp.pad(mods, ((0, 0), (0, 0), (0, MOD_ROWS - 9), (0, 0)))


def _gelu_tanh(g):
    return 0.5 * g * (1.0 + jnp.tanh(np.sqrt(2.0 / np.pi).astype(np.float32) * (g + 0.044715 * (g * g * g))))


def _ffn_kernel(*refs, sub, mixer, final):
    x_ref, mod_ref, ng_ref, wgu_ref, wd_ref = refs[:5]
    rest = list(refs[5:])
    if mixer == "attn":
        a_ref, wo_ref = rest[:2]
        rest = rest[2:]
        mixed = a_ref[...]
    elif mixer == "lru":
        hf_ref, hbk_ref, g_ref, wo_ref = rest[:4]
        rest = rest[4:]
        h = jnp.concatenate(
            [jnp.concatenate([_from_scan_order(hf_ref[n, ch * LRU_T:(ch + 1) * LRU_T, :]
                                               + hbk_ref[n, ch * LRU_T:(ch + 1) * LRU_T, :])
                              for ch in range(CHUNKS_PER_TILE)], axis=0)
             for n in range(LRU_BLOCKS)], axis=1)
        mixed = (h * _gelu_tanh(g_ref[...])).astype(BF16)
    elif mixer == "join":
        ctx_ref = rest.pop(0)
    fg_ref = rest.pop(0) if final else None
    o_ref, hb_ref, acc_ref, xm_ref = rest
    if mixer is None:
        xm_ref = x_ref
    elif mixer == "join":
        xm_ref[...] = jnp.where(pl.program_id(0) < X_TILES, x_ref[...], ctx_ref[...])
    else:
        y = jnp.dot(mixed, wo_ref[...], preferred_element_type=F32)
        xm_ref[...] = x_ref[...] + mod_ref[5:6, :] * y
    hb_ref[...] = _adaln(xm_ref[...], mod_ref, ng_ref, sub).astype(BF16)
    acc_ref[...] = jnp.zeros_like(acc_ref)

    for c in range(N_FF_CHUNKS):
        lo = c * FF_CHUNK
        g = jnp.dot(hb_ref[...], wgu_ref[:, lo:lo + FF_CHUNK], preferred_element_type=F32)
        u = jnp.dot(hb_ref[...], wgu_ref[:, D_FF + lo:D_FF + lo + FF_CHUNK], preferred_element_type=F32)
        a = ((g * jax.nn.sigmoid(g)) * u).astype(BF16)
        acc_ref[...] += jnp.dot(a, wd_ref[lo:lo + FF_CHUNK, :], preferred_element_type=F32)
    gate = mod_ref[3 * sub + 2:3 * sub + 3, :]
    y = xm_ref[...] + (FFN_RES * gate) * acc_ref[...]
    if final:
        y = (y * lax.rsqrt(jnp.mean(y * y, axis=-1, keepdims=True) + EPS)) * fg_ref[...]
    o_ref[...] = y


def _ffn(xs, mods_i, ng_i, wgu, wd, layer, which, sub, ctx=None, attn=None, lru=None, w_o=None, final_g=None):
    final = final_g is not None
    n_tiles = X_TILES if final else N_TILES
    w_map = lambda i: (layer, which, 0, 0)
    row_spec = lambda cols: pl.BlockSpec((TM, cols), lambda i: (i, 0))
    x_spec = row_spec(D_MODEL)
    if ctx is not None:
        x_spec = pl.BlockSpec((TM, D_MODEL), lambda i: (jnp.minimum(i, X_TILES - 1), 0))
    in_specs = [
        x_spec,
        pl.BlockSpec((None, MOD_ROWS, D_MODEL), lambda i: (_tile_kind(i), 0, 0)),
        _const_spec((SUBLANES, D_MODEL)),
        pl.BlockSpec((None, None, D_MODEL, 2 * D_FF), w_map, pipeline_mode=pl.Buffered(1)),
        pl.BlockSpec((None, None, D_FF, D_MODEL), w_map, pipeline_mode=pl.Buffered(1)),
    ]
    args = [xs, mods_i, ng_i, wgu, wd]
    mixer = None
    if attn is not None:
        mixer = "attn"
        in_specs += [row_spec(D_MODEL), _const_spec((D_MODEL, D_MODEL))]
        args += [attn, w_o]
    elif lru is not None:
        mixer = "lru"
        h_spec = pl.BlockSpec((LRU_BLOCKS, TM, LRU_BW), lambda i: (0, i, 0))
        in_specs += [h_spec, h_spec, row_spec(D_RNN), _const_spec((D_RNN, D_MODEL))]
        args += [*lru, w_o]
    elif ctx is not None:
        mixer = "join"
        in_specs.append(_const_spec((N_CTX, D_MODEL)))
        args.append(ctx)
    if final:
        in_specs.append(_const_spec((1, D_MODEL)))
        args.append(final_g.reshape(1, D_MODEL))
    scratch = [pltpu.VMEM((TM, D_MODEL), BF16), pltpu.VMEM((TM, D_MODEL), F32), pltpu.VMEM((TM, D_MODEL), F32)]
    return pl.pallas_call(
        functools.partial(_ffn_kernel, sub=sub, mixer=mixer, final=final),
        grid=(n_tiles,),
        in_specs=in_specs,
        out_specs=row_spec(D_MODEL),
        out_shape=jax.ShapeDtypeStruct((n_tiles * TM, D_MODEL), F32),
        scratch_shapes=scratch,
        compiler_params=_cparams(("arbitrary",)),
        name="ffn_" + (mixer or "plain") + ("_final" if final else ""),
    )(*args)


NMM_COLS = 512


def _rope_tile(y, c, s1, s2):
    return y * c + pltpu.roll(y, LANES - 16, 1) * s1 + pltpu.roll(y, 16, 1) * s2


def _nmm_kernel(*refs, n_cols, rope_cols, out_dtype):
    if rope_cols:
        x_ref, mod_ref, ng_ref, w_ref, rt_ref, ct_ref, o_ref = refs
        is_x = pl.program_id(0) < X_TILES

        def factor(k, ident):
            t = jnp.concatenate([ct_ref[k] + rt_ref[k, g:g + 1, :] for g in range(TM // GRID_W)], axis=0)
            return jnp.where(is_x, t, ident)

        c, s1, s2 = factor(0, 1.0), factor(1, 0.0), factor(2, 0.0)
    else:
        x_ref, mod_ref, ng_ref, w_ref, o_ref = refs
    hb = _adaln(x_ref[...], mod_ref, ng_ref, 1).astype(BF16)
    for c0 in range(0, n_cols, NMM_COLS):
        y = jnp.dot(hb, w_ref[:, c0:c0 + NMM_COLS], preferred_element_type=F32)
        if c0 < rope_cols:
            y = jnp.concatenate(
                [_rope_tile(y[:, t:t + LANES], c, s1, s2) for t in range(0, NMM_COLS, LANES)], axis=1)
        o_ref[:, c0:c0 + NMM_COLS] = y.astype(out_dtype)


def _nmm(xs, mods_i, ng_i, w, out_dtype, rope=None, rope_cols=0):
    n_cols = w.shape[1]
    in_specs = [
        pl.BlockSpec((TM, D_MODEL), lambda i: (i, 0)),
        pl.BlockSpec((None, MOD_ROWS, D_MODEL), lambda i: (_tile_kind(i), 0, 0)),
        _const_spec((SUBLANES, D_MODEL)),
        _const_spec((D_MODEL, n_cols)),
    ]
    args = [xs, mods_i, ng_i, w]
    if rope_cols:
        rows_per_tile = TM // GRID_W
        rmap = lambda i: (0, jnp.where(i < X_TILES, i % TILES_PER_BATCH, 0), 0)
        in_specs += [pl.BlockSpec((3, rows_per_tile, LANES), rmap), _const_spec((3, GRID_W, LANES))]
        args += list(rope)
    return pl.pallas_call(
        functools.partial(_nmm_kernel, n_cols=n_cols, rope_cols=rope_cols, out_dtype=out_dtype),
        grid=(N_TILES,),
        in_specs=in_specs,
        out_specs=pl.BlockSpec((TM, n_cols), lambda i: (i, 0)),
        out_shape=jax.ShapeDtypeStruct((N_ALL, n_cols), out_dtype),
        compiler_params=_cparams(("arbitrary",)),
        name="adaln_proj",
    )(*args)


def _rope_tables():
    quarter = HEAD_DIM // 4
    inv = ROPE_BASE ** (-jnp.arange(quarter, dtype=F32) / quarter)
    heads = LANES // HEAD_DIM

    def tables(n, row_half):
        ang = jnp.arange(n).astype(F32)[:, None] * inv
        c, s = jnp.cos(ang), jnp.sin(ang)
        z = jnp.zeros_like(s)
        lay = lambda a, b: jnp.concatenate(([a, b, z, z] if row_half else [z, z, a, b]) * heads, axis=1)
        return jnp.stack([lay(c, c), lay(-s, z), lay(z, s)])

    return tables(GRID_ROWS, True), tables(GRID_W, False)


LRU_T = 256
LRU_SEG = LRU_T // SUBLANES
X_CHUNKS = SEQ // LRU_T
HALO = SUBLANES
CHUNKS_PER_TILE = TM // LRU_T
PROJ_GROUP = 256


def _to_scan_order(v):
    return jnp.swapaxes(v.reshape(SUBLANES, LRU_SEG, LRU_BW), 0, 1).reshape(LRU_T, LRU_BW)


def _from_scan_order(v):
    return jnp.swapaxes(v.reshape(LRU_SEG, SUBLANES, LRU_BW), 0, 1).reshape(LRU_T, LRU_BW)


def _lru_in_kernel(x_ref, xp_ref, xn_ref, mod_ref, ng_ref, w_ref, cw_ref, g_ref, xc_ref, hb_ref):
    i = pl.program_id(0)
    is_ctx = i >= X_TILES
    first = i % TILES_PER_BATCH == 0
    last = i % TILES_PER_BATCH == TILES_PER_BATCH - 1
    hb_ref[0:TM, :] = _adaln(x_ref[...], mod_ref, ng_ref, 1).astype(BF16)
    halo = jnp.concatenate([xp_ref[...], xn_ref[...]], axis=0)
    hb_ref[TM:, :] = _adaln(halo, mod_ref, ng_ref, 1).astype(BF16)
    sub = lax.broadcasted_iota(jnp.int32, (SUBLANES, LRU_BW), 0)
    pad_before = [jnp.logical_or(is_ctx, first) if ch == 0 else is_ctx for ch in range(CHUNKS_PER_TILE)]
    pad_after = [jnp.logical_or(is_ctx, last) if ch == CHUNKS_PER_TILE - 1 else is_ctx
                 for ch in range(CHUNKS_PER_TILE)]

    def conv(n, ch, chunk, before, after):
        cols = slice(n * LRU_BW, (n + 1) * LRU_BW)
        cb = cw_ref[CONV_W:CONV_W + 1, cols]
        taps = [cw_ref[j:j + 1, cols] for j in range(CONV_W)]
        before = jnp.where(pad_before[ch], 0.0, before)
        after = jnp.where(pad_after[ch], 0.0, after)
        xp = _to_scan_order(chunk)
        xg = [xp[k * SUBLANES:(k + 1) * SUBLANES] for k in range(LRU_SEG)]
        xm1 = jnp.where(sub == 0, before, pltpu.roll(xg[LRU_SEG - 1], 1, 0))
        xp0 = jnp.where(sub == SUBLANES - 1, after[0:1], pltpu.roll(xg[0], SUBLANES - 1, 0))
        xp1 = jnp.where(sub == SUBLANES - 1, after[1:2], pltpu.roll(xg[1], SUBLANES - 1, 0))
        xe = [xm1] + xg + [xp0, xp1]
        for k in range(LRU_SEG):
            xc = cb + xe[k] * taps[0] + xe[k + 1] * taps[1] + xe[k + 2] * taps[2] + xe[k + 3] * taps[3]
            xc_ref[n, ch * LRU_T + k * SUBLANES:ch * LRU_T + (k + 1) * SUBLANES, :] = xc

    for c0 in range(0, D_RNN, PROJ_GROUP):
        wx = w_ref[:, D_RNN + c0:D_RNN + c0 + PROJ_GROUP]
        y = jnp.dot(hb_ref[...], wx, preferred_element_type=F32)
        g_ref[:, c0:c0 + PROJ_GROUP] = jnp.dot(hb_ref[0:TM, :], w_ref[:, c0:c0 + PROJ_GROUP],
                                               preferred_element_type=F32)
        for t in range(PROJ_GROUP // LRU_BW):
            lanes = slice(t * LRU_BW, (t + 1) * LRU_BW)
            ext = jnp.concatenate([y[TM:TM + HALO, lanes], y[0:TM, lanes], y[TM + HALO:, lanes]], axis=0)
            for ch in range(CHUNKS_PER_TILE):
                lo = HALO + ch * LRU_T
                conv(c0 // LRU_BW + t, ch, ext[lo:lo + LRU_T], ext[lo - 1:lo], ext[lo + LRU_T:lo + LRU_T + 2])


def _lru_in_proj(xs, mods_i, ng_i, w, cw):
    blocks_per_tile = TM // HALO
    n_halo_blocks = N_ALL // HALO
    return pl.pallas_call(
        _lru_in_kernel,
        grid=(N_TILES,),
        in_specs=[
            pl.BlockSpec((TM, D_MODEL), lambda i: (i, 0)),
            pl.BlockSpec((HALO, D_MODEL), lambda i: (jnp.maximum(i * blocks_per_tile - 1, 0), 0)),
            pl.BlockSpec((HALO, D_MODEL), lambda i: (jnp.minimum((i + 1) * blocks_per_tile, n_halo_blocks - 1), 0)),
            pl.BlockSpec((None, MOD_ROWS, D_MODEL), lambda i: (_tile_kind(i), 0, 0)),
            _const_spec((SUBLANES, D_MODEL)),
            _const_spec((D_MODEL, 2 * D_RNN)),
            _const_spec((SUBLANES, D_RNN)),
        ],
        out_specs=[pl.BlockSpec((TM, D_RNN), lambda i: (i, 0)),
                   pl.BlockSpec((LRU_BLOCKS, TM, LRU_BW), lambda i: (0, i, 0))],
        out_shape=[jax.ShapeDtypeStruct((N_ALL, D_RNN), F32),
                   jax.ShapeDtypeStruct((LRU_BLOCKS, N_ALL, LRU_BW), F32)],
        scratch_shapes=[pltpu.VMEM((TM + 2 * HALO, D_MODEL), BF16)],
        compiler_params=_cparams(("arbitrary",)),
        name="lru_in_proj",
    )(xs, xs, xs, mods_i, ng_i, w, cw)


def _softplus(x):
    return jnp.maximum(x, 0.0) + jnp.log1p(jnp.exp(-jnp.abs(x)))


def _lru_kernel(xf_ref, xb_ref, gwf_ref, gwb_ref, gbf_ref, gbb_ref, of_ref, ob_ref,
                h_ref, p_ref, carry_ref):
    @pl.when(pl.program_id(1) == 0)
    def _():
        carry_ref[...] = jnp.zeros_like(carry_ref)

    for n in range(LRU_BLOCKS):
        _lru_block(n, xf_ref, gwf_ref, gbf_ref, of_ref, h_ref.at[0], p_ref.at[0], carry_ref.at[0], False)
        _lru_block(n, xb_ref, gwb_ref, gbb_ref, ob_ref, h_ref.at[1], p_ref.at[1], carry_ref.at[1], True)


def _lru_block(n, xc_ref, gw_ref, gb_ref, o_ref, h_ref, p_ref, carry_ref, reverse):
    sub = lax.broadcasted_iota(jnp.int32, (SUBLANES, LRU_BW), 0)
    steps = range(LRU_SEG - 1, -1, -1) if reverse else range(LRU_SEG)
    segs = range(SUBLANES - 1, -1, -1) if reverse else range(SUBLANES)

    cols = slice(n * LRU_BW, (n + 1) * LRU_BW)
    xc = xc_ref[n]
    ri = jnp.dot(xc.astype(BF16), gw_ref[n], preferred_element_type=F32)
    r2 = jnp.tanh(ri[:, :LRU_BW] + gb_ref[0:1, cols]) + 1.0
    i2 = jnp.tanh(ri[:, LRU_BW:] + gb_ref[1:2, cols]) + 1.0
    log_a = r2 * ((-0.5 * LRU_C) * _softplus(-gb_ref[2:3, cols]))
    a = jnp.exp(log_a)
    t = jnp.tanh(log_a)
    z = (-0.5 * t) / (1.0 - t)
    b = jnp.where(z > 0.0, z * lax.rsqrt(z), 0.0) * (i2 * xc)

    h = jnp.zeros((SUBLANES, LRU_BW), F32)
    p = jnp.ones((SUBLANES, LRU_BW), F32)
    for k in steps:
        grp = slice(k * SUBLANES, (k + 1) * SUBLANES)
        h = a[grp] * h + b[grp]
        p = a[grp] * p
        h_ref[n, grp, :] = h
        p_ref[n, grp, :] = p
    st = carry_ref[n, 0:1, :]
    enter = jnp.zeros((SUBLANES, LRU_BW), F32)
    for s in segs:
        enter = jnp.where(sub == s, st, enter)
        st = p[s:s + 1, :] * st + h[s:s + 1, :]
    carry_ref[n, 0:1, :] = st
    for k in range(LRU_SEG):
        grp = slice(k * SUBLANES, (k + 1) * SUBLANES)
        o_ref[n, grp, :] = h_ref[n, grp, :] + p_ref[n, grp, :] * enter


def _lru_scan(xc, gw, gb):
    def row_block(reverse):
        def f(b, m):
            c = (X_CHUNKS - m) if reverse else (m - 1)
            return (0, jnp.where(m == 0, N_X // LRU_T + b, b * X_CHUNKS + c), 0)
        return f

    chunk = lambda reverse: pl.BlockSpec((LRU_BLOCKS, LRU_T, LRU_BW), row_block(reverse))
    gw_spec = _const_spec((LRU_BLOCKS, LRU_BW, 2 * LRU_BW))
    gb_spec = _const_spec((SUBLANES, D_RNN))
    out = jax.ShapeDtypeStruct((LRU_BLOCKS, N_ALL, LRU_BW), F32)
    return pl.pallas_call(
        _lru_kernel,
        grid=(BATCH, X_CHUNKS + 1),
        in_specs=[chunk(False), chunk(True), gw_spec, gw_spec, gb_spec, gb_spec],
        out_specs=[chunk(False), chunk(True)],
        out_shape=[out, out],
        scratch_shapes=[
            pltpu.VMEM((2, LRU_BLOCKS, LRU_T, LRU_BW), F32),
            pltpu.VMEM((2, LRU_BLOCKS, LRU_T, LRU_BW), F32),
            pltpu.VMEM((2, LRU_BLOCKS, SUBLANES, LRU_BW), F32),
        ],
        compiler_params=_cparams(("arbitrary", "arbitrary")),
        name="lru_scan",
    )(xc, xc, gw[0], gw[1], gb[0], gb[1])


SWA_TQ = 128
SWA_XQ = SEQ // SWA_TQ
SWA_QB = CTX_LEN // SWA_TQ
SWA_STEP = SWA_QB * SWA_TQ
SWA_XSTEPS = SEQ // SWA_STEP
SWA_SPAN = 3 * SWA_TQ
KV_COLS = 2 * B_KV_HEADS * LANES
_NT = (((1,), (1,)), ((), ()))


def _split_heads(qt):
    lo = lax.broadcasted_iota(jnp.int32, qt.shape, 1) < HEAD_DIM
    zero = jnp.zeros_like(qt)
    return jnp.concatenate([jnp.where(lo, qt, zero), jnp.where(lo, zero, qt)], axis=0)


def _merge_heads(o):
    t = o.shape[0] // 2
    lo = lax.broadcasted_iota(jnp.int32, (t, LANES), 1) < HEAD_DIM
    return jnp.where(lo, o[:t], o[t:])


def _swa_kernel(sink_ref, q_ref, kvp_ref, kvc_ref, kvn_ref, kvx_ref, o_ref):
    j = pl.program_id(1)
    is_x = j < SWA_XSTEPS
    iq = lax.broadcasted_iota(jnp.int32, (SWA_TQ, SWA_SPAN), 0)
    ik = lax.broadcasted_iota(jnp.int32, (SWA_TQ, SWA_SPAN), 1)
    band = jnp.where(jnp.abs(iq + WINDOW - ik) <= WINDOW, 0.0, NEG)
    biases = []
    for u in range(SWA_QB):
        jb = j * SWA_QB + u
        lo_ok = jnp.where(jb > 0, 0, SWA_TQ)
        hi_ok = jnp.where(jb < SWA_XQ - 1, SWA_SPAN, 2 * SWA_TQ)
        hi_ok = jnp.where(is_x, hi_ok, 0)
        b1 = jnp.where(ik >= lo_ok, band, NEG)
        b1 = jnp.where(ik < hi_ok, b1, NEG)
        biases.append(jnp.concatenate([b1, b1], axis=0))
    top = lax.broadcasted_iota(jnp.int32, (2 * SWA_TQ, 1), 0) < SWA_TQ
    for g in range(B_KV_HEADS):
        kcol = slice(g * LANES, (g + 1) * LANES)
        vcol = slice((B_KV_HEADS + g) * LANES, (B_KV_HEADS + g + 1) * LANES)
        own = [slice(u * SWA_TQ, (u + 1) * SWA_TQ) for u in range(SWA_QB)]
        kb = [kvp_ref[:, kcol]] + [kvc_ref[r, kcol] for r in own] + [kvn_ref[:, kcol]]
        vb = [kvp_ref[:, vcol]] + [kvc_ref[r, vcol] for r in own] + [kvn_ref[:, vcol]]
        kx = kvx_ref[:, kcol]
        vx = kvx_ref[:, vcol]
        for u in range(SWA_QB):
            kd = jnp.concatenate(kb[u:u + 3], axis=0)
            vd = jnp.concatenate(vb[u:u + 3], axis=0)
            for t in range(2):
                tile = 2 * g + t
                cols = slice(tile * LANES, (tile + 1) * LANES)
                qs = _split_heads(q_ref[own[u], cols])
                s_loc = lax.dot_general(qs, kd, _NT, preferred_element_type=F32) + biases[u]
                s_ctx = lax.dot_general(qs, kx, _NT, preferred_element_type=F32)
                sink = jnp.where(top, sink_ref[2 * tile], sink_ref[2 * tile + 1])
                mx = jnp.maximum(jnp.maximum(jnp.max(s_loc, axis=-1, keepdims=True),
                                             jnp.max(s_ctx, axis=-1, keepdims=True)), sink)
                p_loc = jnp.exp(s_loc - mx)
                p_ctx = jnp.exp(s_ctx - mx)
                den = (jnp.sum(p_loc, axis=-1, keepdims=True) + jnp.sum(p_ctx, axis=-1, keepdims=True)
                       + jnp.exp(sink - mx))
                o = (jnp.dot(p_loc.astype(BF16), vd, preferred_element_type=F32)
                     + jnp.dot(p_ctx.astype(BF16), vx, preferred_element_type=F32)) / den
                o_ref[own[u], cols] = _merge_heads(o).astype(BF16)


def _swa_attention(qkv, sinks):
    def q_block(b, j):
        return jnp.where(j < SWA_XSTEPS, b * SWA_XSTEPS + j, N_X // SWA_STEP + b)

    def edge_block(off):
        return lambda b, j: (b * SWA_XQ + jnp.clip(j * SWA_QB + off, 0, SWA_XQ - 1), 1)

    return pl.pallas_call(
        _swa_kernel,
        grid=(BATCH, SWA_XSTEPS + 1),
        in_specs=[
            pl.BlockSpec(memory_space=pltpu.SMEM),
            pl.BlockSpec((SWA_STEP, D_MODEL), lambda b, j: (q_block(b, j), 0)),
            pl.BlockSpec((SWA_TQ, KV_COLS), edge_block(-1)),
            pl.BlockSpec((SWA_STEP, KV_COLS), lambda b, j: (b * SWA_XSTEPS + jnp.minimum(j, SWA_XSTEPS - 1), 1)),
            pl.BlockSpec((SWA_TQ, KV_COLS), edge_block(SWA_QB)),
            pl.BlockSpec((CTX_LEN, KV_COLS), lambda b, j: (N_X // CTX_LEN + b, 1)),
        ],
        out_specs=pl.BlockSpec((SWA_STEP, D_MODEL), lambda b, j: (q_block(b, j), 0)),
        out_shape=jax.ShapeDtypeStruct((N_ALL, D_MODEL), BF16),
        compiler_params=_cparams(("arbitrary", "arbitrary")),
        name="swa_attn",
    )(sinks, qkv, qkv, qkv, qkv, qkv)


def _prep_swa_weights(w_qkv):
    nq = B_HEADS * HEAD_DIM
    nk = B_KV_HEADS * HEAD_DIM
    wq = w_qkv[:, :nq] * (HEAD_DIM ** -0.5)
    dup = lambda w: jnp.tile(w.reshape(D_MODEL, B_KV_HEADS, 1, HEAD_DIM), (1, 1, 2, 1)).reshape(D_MODEL, -1)
    wk = dup(w_qkv[:, nq:nq + nk])
    wv = dup(w_qkv[:, nq + nk:])
    return jnp.concatenate([wq, wk, wv], axis=1).astype(BF16)


NAT_ROWS = 4
NAT_TQ = NAT_ROWS * GRID_W
NAT_XQ = SEQ // NAT_TQ
NAT_SPAN = 3 * NAT_TQ
N_HEAD_TILES = C_HEADS * HEAD_DIM // LANES
GRID_ROWS = SEQ // GRID_W


NAT_KEY_ROWS = NAT_SPAN // GRID_W
NAT_RPB_ROWS = 2 * WIN_H
NAT_PATTERNS = 4


def _nat_pattern(j):
    return jnp.where(j == 0, 0, jnp.where(j == NAT_XQ - 1, 2, jnp.where(j == NAT_XQ, 3, 1)))


def _nat_kernel(sel_ref, q_ref, kp_ref, kc_ref, kn_ref, vp_ref, vc_ref, vn_ref, kx_ref, vx_ref, tab_ref,
                o_ref, bias_ref):
    j = pl.program_id(1)

    @pl.when(jnp.logical_or(j <= 1, j >= NAT_XQ - 1))
    def _():
        lo = lax.broadcasted_iota(jnp.int32, (GRID_W, LANES), 1) < GRID_W
        base = _nat_pattern(j) * (NAT_ROWS * NAT_KEY_ROWS)
        for qr in range(NAT_ROWS):
            rows = slice(qr * GRID_W, (qr + 1) * GRID_W)
            for kp in range(NAT_KEY_ROWS // 2):
                e0 = sel_ref[base + qr * NAT_KEY_ROWS + 2 * kp]
                e1 = sel_ref[base + qr * NAT_KEY_ROWS + 2 * kp + 1]

                def fill(h, carry, rows=rows, kp=kp, e0=e0, e1=e1):
                    bias_ref[h, rows, kp * LANES:(kp + 1) * LANES] = jnp.where(lo, tab_ref[h, e0], tab_ref[h, e1])
                    return carry

                lax.fori_loop(0, C_HEADS, fill, 0)

    for t in range(N_HEAD_TILES):
        cols = slice(t * LANES, (t + 1) * LANES)
        qs = _split_heads(q_ref[:, cols])
        kd = jnp.concatenate([kp_ref[:, cols], kc_ref[:, cols], kn_ref[:, cols]], axis=0)
        vd = jnp.concatenate([vp_ref[:, cols], vc_ref[:, cols], vn_ref[:, cols]], axis=0)
        bias = bias_ref[2 * t:2 * t + 2].reshape(2 * NAT_TQ, NAT_SPAN)
        s_loc = lax.dot_general(qs, kd, _NT, preferred_element_type=F32) + bias
        s_ctx = lax.dot_general(qs, kx_ref[:, cols], _NT, preferred_element_type=F32)
        mx = jnp.maximum(jnp.max(s_loc, axis=-1, keepdims=True), jnp.max(s_ctx, axis=-1, keepdims=True))
        p_loc = jnp.exp(s_loc - mx)
        p_ctx = jnp.exp(s_ctx - mx)
        den = jnp.sum(p_loc, axis=-1, keepdims=True) + jnp.sum(p_ctx, axis=-1, keepdims=True)
        o = (jnp.dot(p_loc.astype(BF16), vd, preferred_element_type=F32)
             + jnp.dot(p_ctx.astype(BF16), vx_ref[:, cols], preferred_element_type=F32)) / den
        o_ref[:, cols] = _merge_heads(o).astype(BF16)


def _nat_attention(qkv, row_sel, tab):
    nq = NAT_XQ + 1

    def q_block(b, j):
        return jnp.where(j < NAT_XQ, b * NAT_XQ + j, N_X // NAT_TQ + b)

    def kv_spec(off, part):
        return pl.BlockSpec(
            (NAT_TQ, D_MODEL), lambda b, j: (b * NAT_XQ + jnp.clip(j + off, 0, NAT_XQ - 1), part))

    def ctx_spec(part):
        return pl.BlockSpec((CTX_LEN, D_MODEL), lambda b, j: (N_X // CTX_LEN + b, part))

    return pl.pallas_call(
        _nat_kernel,
        grid=(BATCH, nq),
        in_specs=[
            pl.BlockSpec(memory_space=pltpu.SMEM),
            pl.BlockSpec((NAT_TQ, D_MODEL), lambda b, j: (q_block(b, j), 0)),
            kv_spec(-1, 1), kv_spec(0, 1), kv_spec(1, 1),
            kv_spec(-1, 2), kv_spec(0, 2), kv_spec(1, 2),
            ctx_spec(1), ctx_spec(2),
            _const_spec((C_HEADS, NAT_RPB_ROWS, GRID_W, LANES)),
        ],
        out_specs=pl.BlockSpec((NAT_TQ, D_MODEL), lambda b, j: (q_block(b, j), 0)),
        out_shape=jax.ShapeDtypeStruct((N_ALL, D_MODEL), BF16),
        scratch_shapes=[pltpu.VMEM((C_HEADS, NAT_TQ, NAT_SPAN), F32)],
        compiler_params=_cparams(("arbitrary", "arbitrary")),
        name="nat_attn",
    )(row_sel, qkv, qkv, qkv, qkv, qkv, qkv, qkv, qkv, qkv, tab)


def _nat_row_select():
    qr = np.arange(NAT_ROWS)
    kr = np.arange(NAT_KEY_ROWS) - NAT_ROWS
    row_sel = []
    for r0 in (0, NAT_ROWS, GRID_ROWS - NAT_ROWS):
        r = r0 + qr
        rs = np.clip(r - WIN_H // 2, 0, GRID_ROWS - WIN_H)
        k_abs = r0 + kr
        row_ok = (k_abs[None, :] >= rs[:, None]) & (k_abs[None, :] < rs[:, None] + WIN_H)
        row_sel.append(np.where(row_ok, k_abs[None, :] - r[:, None] + (WIN_H - 1), NAT_RPB_ROWS - 1))
    row_sel.append(np.full((NAT_ROWS, NAT_KEY_ROWS), NAT_RPB_ROWS - 1))
    return jnp.asarray(np.stack(row_sel).reshape(-1), jnp.int32)


def _nat_bias_tiles(rpb):
    col = np.arange(GRID_W)
    col_start = np.clip(col - WIN_W // 2, 0, GRID_W - WIN_W)
    col_ok = (col[None, :] >= col_start[:, None]) & (col[None, :] < col_start[:, None] + WIN_W)
    col_off = np.clip(col[None, :] - col[:, None] + (WIN_W - 1), 0, 2 * WIN_W - 2)
    by_col = jnp.where(col_ok[None, None], rpb[:, :, col_off], NEG)
    by_col = jnp.concatenate([by_col, jnp.full((C_HEADS, 1, GRID_W, GRID_W), NEG, F32)], axis=1)
    return jnp.concatenate([by_col, by_col], axis=-1)


def _prep_nat_weights(w_qkv):
    nq = C_HEADS * HEAD_DIM
    return jnp.concatenate([w_qkv[:, :nq] * (HEAD_DIM ** -0.5), w_qkv[:, nq:]], axis=1).astype(BF16)


def _pad_rows(a, rows=SUBLANES):
    return jnp.pad(a, ((0, rows - a.shape[0]), (0, 0)))


def kernel(x, c, ctx, c_ctx, w_ada, b_ada, norm_g, w_ffn_gu, w_ffn_down, a_w_in, a_conv_w, a_conv_b,
           a_gate_w, a_gate_b, a_lambda, a_w_out, b_w_qkv, b_sinks, b_w_o, c_w_qkv, c_rpb, c_w_o, final_g):
    xs = x.reshape(N_X, D_MODEL)
    cvec = _pad_rows(jnp.concatenate([c, c_ctx[None, :]], axis=0))
    mods = _ada_mods(cvec, w_ada, b_ada)
    rope = _rope_tables()
    wgu, wd = w_ffn_gu.astype(BF16), w_ffn_down.astype(BF16)
    for i in range(DEPTH):
        kind, j = i % N_MIXERS, i // N_MIXERS
        last = i == DEPTH - 1
        ng = _pad_rows(norm_g[i])
        xs = _ffn(xs, mods[i], ng, wgu, wd, i, 0, 0, ctx=ctx.reshape(N_CTX, D_MODEL) if i == 0 else None)
        if kind == 0:
            cw = _pad_rows(jnp.concatenate([a_conv_w[j], a_conv_b[j][None, :]], axis=0))
            g_branch, xc = _lru_in_proj(xs, mods[i], ng, a_w_in[j].astype(BF16), cw)
            gb = [_pad_rows(jnp.concatenate([0.5 * a_gate_b[j, d], a_lambda[j, d][None, :]], axis=0))
                  for d in range(2)]
            gw = [(0.5 * jnp.concatenate([a_gate_w[j, d, 0], a_gate_w[j, d, 1]], axis=-1)).astype(BF16)
                  for d in range(2)]
            hf, hb = _lru_scan(xc, gw, gb)
            mixed = dict(lru=(hf, hb, g_branch), w_o=a_w_out[j].astype(BF16))
        elif kind == 1:
            qkv = _nmm(xs, mods[i], ng, _prep_swa_weights(b_w_qkv[j]), BF16, rope=rope,
                       rope_cols=B_HEADS * HEAD_DIM + B_KV_HEADS * LANES)
            mixed = dict(attn=_swa_attention(qkv, b_sinks[j]), w_o=b_w_o[j].astype(BF16))
        else:
            qkv = _nmm(xs, mods[i], ng, _prep_nat_weights(c_w_qkv[j]), BF16)
            o = _nat_attention(qkv, _nat_row_select(), _nat_bias_tiles(c_rpb[j]))
            mixed = dict(attn=o, w_o=c_w_o[j].astype(BF16))
        xs = _ffn(xs, mods[i], ng, wgu, wd, i, 1, 2, final_g=final_g if last else None, **mixed)
    return xs.reshape(BATCH, SEQ, D_MODEL)
```

```python
import functools

import numpy as np
import jax
import jax.numpy as jnp
from jax import lax
from jax.experimental import pallas as pl
from jax.experimental.pallas import tpu as pltpu

D_MODEL = 1024
BATCH = 2
SEQ = 16384
DEPTH = 4
GRID_W = 64
CTX_LEN = 256
N_MIXERS = 3
EPS = 1e-6
FFN_RES = 0.5
D_FF = 2816
D_RNN = 1536
LRU_BLOCKS = 12
LRU_BW = D_RNN // LRU_BLOCKS
CONV_W = 4
LRU_C = 8.0
HEAD_DIM = 64
B_HEADS = 16
B_KV_HEADS = 4
WINDOW = 128
ROPE_BASE = 10000.0
C_HEADS = 16
WIN_H = 8
WIN_W = 16
NEG = -1e30

LANES = 128
SUBLANES = 8
VMEM_LIMIT = 56 * 1024 * 1024

N_X = BATCH * SEQ
N_CTX = BATCH * CTX_LEN
N_ALL = N_X + N_CTX
TM = 512
N_TILES = N_ALL // TM
X_TILES = N_X // TM
TILES_PER_BATCH = SEQ // TM
MOD_ROWS = 16
FF_CHUNK = 256
N_FF_CHUNKS = D_FF // FF_CHUNK

F32 = jnp.float32
BF16 = jnp.bfloat16


def _cparams(sem):
    return pltpu.CompilerParams(dimension_semantics=sem, vmem_limit_bytes=VMEM_LIMIT)


def _const_spec(shape):
    nd = len(shape)
    return pl.BlockSpec(shape, lambda *_: (0,) * nd, pipeline_mode=pl.Buffered(1))


def _tile_kind(i):
    return jnp.minimum(i // TILES_PER_BATCH, BATCH)


def _adaln(x, mod_ref, ng_ref, sub):
    g = ng_ref[sub:sub + 1, :]
    shift = mod_ref[3 * sub:3 * sub + 1, :]
    scale = mod_ref[3 * sub + 1:3 * sub + 2, :]
    y = x * lax.rsqrt(jnp.mean(x * x, axis=-1, keepdims=True) + EPS)
    return (y * g) * (1.0 + scale) + shift


ADA_COLS = 1152


def _ada_kernel(c_ref, w_ref, b_ref, o_ref):
    c = c_ref[...]
    sc = c * jax.nn.sigmoid(c)
    o_ref[...] = jnp.dot(sc, w_ref[...], preferred_element_type=F32,
                         precision=lax.Precision.HIGHEST) + b_ref[...]


def _ada_mods(cvec, w_ada, b_ada):
    out = pl.pallas_call(
        _ada_kernel,
        grid=(DEPTH, 9 * D_MODEL // ADA_COLS),
        in_specs=[
            pl.BlockSpec((SUBLANES, D_MODEL), lambda i, j: (0, 0)),
            pl.BlockSpec((None, D_MODEL, ADA_COLS), lambda i, j: (i, 0, j)),
            pl.BlockSpec((None, 1, ADA_COLS), lambda i, j: (i, 0, j)),
        ],
        out_specs=pl.BlockSpec((None, SUBLANES, ADA_COLS), lambda i, j: (i, 0, j)),
        out_shape=jax.ShapeDtypeStruct((DEPTH, SUBLANES, 9 * D_MODEL), F32),
        compiler_params=_cparams(("arbitrary", "arbitrary")),
        name="ada_mods",
    )(cvec, w_ada, b_ada.reshape(DEPTH, 1, 9 * D_MODEL))
    mods = out.reshape(DEPTH, SUBLANES, 9, D_MODEL)[:, :BATCH + 1]
    return jnp.pad(mods, ((0, 0), (0, 0), (0, MOD_ROWS - 9), (0, 0)))


def _gelu_tanh(g):
    return 0.5 * g * (1.0 + jnp.tanh(np.sqrt(2.0 / np.pi).astype(np.float32) * (g + 0.044715 * (g * g * g))))


def _ffn_kernel(*refs, sub, mixer, final):
    x_ref, mod_ref, ng_ref, wgu_ref, wd_ref = refs[:5]
    rest = list(refs[5:])
    if mixer == "attn":
        a_ref, wo_ref = rest[:2]
        rest = rest[2:]
        mixed = a_ref[...]
    elif mixer == "lru":
        hf_ref, hbk_ref, g_ref, wo_ref = rest[:4]
        rest = rest[4:]
        h = jnp.concatenate(
            [jnp.concatenate([_from_scan_order(hf_ref[n, ch * LRU_T:(ch + 1) * LRU_T, :]
                                               + hbk_ref[n, ch * LRU_T:(ch + 1) * LRU_T, :])
                              for ch in range(CHUNKS_PER_TILE)], axis=0)
             for n in range(LRU_BLOCKS)], axis=1)
        mixed = (h * g_ref[...]).astype(BF16)
    elif mixer == "join":
        ctx_ref = rest.pop(0)
    fg_ref = rest.pop(0) if final else None
    o_ref, hb_ref, acc_ref, xm_ref = rest
    if mixer is None:
        xm_ref = x_ref
    elif mixer == "join":
        xm_ref[...] = jnp.where(pl.program_id(0) < X_TILES, x_ref[...], ctx_ref[...])
    else:
        y = jnp.dot(mixed, wo_ref[...], preferred_element_type=F32)
        xm_ref[...] = x_ref[...] + mod_ref[5:6, :] * y
    hb_ref[...] = _adaln(xm_ref[...], mod_ref, ng_ref, sub).astype(BF16)
    acc_ref[...] = jnp.zeros_like(acc_ref)

    for c in range(N_FF_CHUNKS):
        lo = c * FF_CHUNK
        g = jnp.dot(hb_ref[...], wgu_ref[:, lo:lo + FF_CHUNK], preferred_element_type=F32)
        u = jnp.dot(hb_ref[...], wgu_ref[:, D_FF + lo:D_FF + lo + FF_CHUNK], preferred_element_type=F32)
        a = ((g * jax.nn.sigmoid(g)) * u).astype(BF16)
        acc_ref[...] += jnp.dot(a, wd_ref[lo:lo + FF_CHUNK, :], preferred_element_type=F32)
    gate = mod_ref[3 * sub + 2:3 * sub + 3, :]
    y = xm_ref[...] + (FFN_RES * gate) * acc_ref[...]
    if final:
        y = (y * lax.rsqrt(jnp.mean(y * y, axis=-1, keepdims=True) + EPS)) * fg_ref[...]
    o_ref[...] = y


def _ffn(xs, mods_i, ng_i, wgu, wd, layer, which, sub, ctx=None, attn=None, lru=None, w_o=None, final_g=None):
    final = final_g is not None
    n_tiles = X_TILES if final else N_TILES
    w_map = lambda i: (layer, which, 0, 0)
    row_spec = lambda cols: pl.BlockSpec((TM, cols), lambda i: (i, 0))
    x_spec = row_spec(D_MODEL)
    if ctx is not None:
        x_spec = pl.BlockSpec((TM, D_MODEL), lambda i: (jnp.minimum(i, X_TILES - 1), 0))
    in_specs = [
        x_spec,
        pl.BlockSpec((None, MOD_ROWS, D_MODEL), lambda i: (_tile_kind(i), 0, 0)),
        _const_spec((SUBLANES, D_MODEL)),
        pl.BlockSpec((None, None, D_MODEL, 2 * D_FF), w_map, pipeline_mode=pl.Buffered(1)),
        pl.BlockSpec((None, None, D_FF, D_MODEL), w_map, pipeline_mode=pl.Buffered(1)),
    ]
    args = [xs, mods_i, ng_i, wgu, wd]
    mixer = None
    if attn is not None:
        mixer = "attn"
        in_specs += [row_spec(D_MODEL), _const_spec((D_MODEL, D_MODEL))]
        args += [attn, w_o]
    elif lru is not None:
        mixer = "lru"
        h_spec = pl.BlockSpec((LRU_BLOCKS, TM, LRU_BW), lambda i: (0, i, 0))
        in_specs += [h_spec, h_spec, row_spec(D_RNN), _const_spec((D_RNN, D_MODEL))]
        args += [*lru, w_o]
    elif ctx is not None:
        mixer = "join"
        in_specs.append(_const_spec((N_CTX, D_MODEL)))
        args.append(ctx)
    if final:
        in_specs.append(_const_spec((1, D_MODEL)))
        args.append(final_g.reshape(1, D_MODEL))
    scratch = [pltpu.VMEM((TM, D_MODEL), BF16), pltpu.VMEM((TM, D_MODEL), F32), pltpu.VMEM((TM, D_MODEL), F32)]
    return pl.pallas_call(
        functools.partial(_ffn_kernel, sub=sub, mixer=mixer, final=final),
        grid=(n_tiles,),
        in_specs=in_specs,
        out_specs=row_spec(D_MODEL),
        out_shape=jax.ShapeDtypeStruct((n_tiles * TM, D_MODEL), F32),
        scratch_shapes=scratch,
        compiler_params=_cparams(("arbitrary",)),
        name="ffn_" + (mixer or "plain") + ("_final" if final else ""),
    )(*args)


NMM_COLS = 512


def _rope_tile(y, c, s1, s2):
    return y * c + pltpu.roll(y, LANES - 16, 1) * s1 + pltpu.roll(y, 16, 1) * s2


def _nmm_kernel(*refs, n_cols, rope_cols, out_dtype):
    if rope_cols:
        x_ref, mod_ref, ng_ref, w_ref, rt_ref, ct_ref, o_ref = refs
        is_x = pl.program_id(0) < X_TILES

        def factor(k, ident):
            t = jnp.concatenate([ct_ref[k] + rt_ref[k, g:g + 1, :] for g in range(TM // GRID_W)], axis=0)
            return jnp.where(is_x, t, ident)

        c, s1, s2 = factor(0, 1.0), factor(1, 0.0), factor(2, 0.0)
    else:
        x_ref, mod_ref, ng_ref, w_ref, o_ref = refs
    hb = _adaln(x_ref[...], mod_ref, ng_ref, 1).astype(BF16)
    for c0 in range(0, n_cols, NMM_COLS):
        y = jnp.dot(hb, w_ref[:, c0:c0 + NMM_COLS], preferred_element_type=F32)
        if c0 < rope_cols:
            y = jnp.concatenate(
                [_rope_tile(y[:, t:t + LANES], c, s1, s2) for t in range(0, NMM_COLS, LANES)], axis=1)
        o_ref[:, c0:c0 + NMM_COLS] = y.astype(out_dtype)


def _nmm(xs, mods_i, ng_i, w, out_dtype, rope=None, rope_cols=0):
    n_cols = w.shape[1]
    in_specs = [
        pl.BlockSpec((TM, D_MODEL), lambda i: (i, 0)),
        pl.BlockSpec((None, MOD_ROWS, D_MODEL), lambda i: (_tile_kind(i), 0, 0)),
        _const_spec((SUBLANES, D_MODEL)),
        _const_spec((D_MODEL, n_cols)),
    ]
    args = [xs, mods_i, ng_i, w]
    if rope_cols:
        rows_per_tile = TM // GRID_W
        rmap = lambda i: (0, jnp.where(i < X_TILES, i % TILES_PER_BATCH, 0), 0)
        in_specs += [pl.BlockSpec((3, rows_per_tile, LANES), rmap), _const_spec((3, GRID_W, LANES))]
        args += list(rope)
    return pl.pallas_call(
        functools.partial(_nmm_kernel, n_cols=n_cols, rope_cols=rope_cols, out_dtype=out_dtype),
        grid=(N_TILES,),
        in_specs=in_specs,
        out_specs=pl.BlockSpec((TM, n_cols), lambda i: (i, 0)),
        out_shape=jax.ShapeDtypeStruct((N_ALL, n_cols), out_dtype),
        compiler_params=_cparams(("arbitrary",)),
        name="adaln_proj",
    )(*args)


def _rope_tables():
    quarter = HEAD_DIM // 4
    inv = ROPE_BASE ** (-jnp.arange(quarter, dtype=F32) / quarter)
    heads = LANES // HEAD_DIM

    def tables(n, row_half):
        ang = jnp.arange(n).astype(F32)[:, None] * inv
        c, s = jnp.cos(ang), jnp.sin(ang)
        z = jnp.zeros_like(s)
        lay = lambda a, b: jnp.concatenate(([a, b, z, z] if row_half else [z, z, a, b]) * heads, axis=1)
        return jnp.stack([lay(c, c), lay(-s, z), lay(z, s)])

    return tables(GRID_ROWS, True), tables(GRID_W, False)


LRU_T = 256
LRU_SEG = LRU_T // SUBLANES
X_CHUNKS = SEQ // LRU_T
HALO = SUBLANES
CHUNKS_PER_TILE = TM // LRU_T
PROJ_GROUP = 256


def _to_scan_order(v):
    return jnp.swapaxes(v.reshape(SUBLANES, LRU_SEG, LRU_BW), 0, 1).reshape(LRU_T, LRU_BW)


def _from_scan_order(v):
    return jnp.swapaxes(v.reshape(LRU_SEG, SUBLANES, LRU_BW), 0, 1).reshape(LRU_T, LRU_BW)


def _lru_in_kernel(x_ref, xp_ref, xn_ref, mod_ref, ng_ref, w_ref, cw_ref, g_ref, xc_ref, hb_ref):
    i = pl.program_id(0)
    is_ctx = i >= X_TILES
    first = i % TILES_PER_BATCH == 0
    last = i % TILES_PER_BATCH == TILES_PER_BATCH - 1
    hb_ref[0:TM, :] = _adaln(x_ref[...], mod_ref, ng_ref, 1).astype(BF16)
    halo = jnp.concatenate([xp_ref[...], xn_ref[...]], axis=0)
    hb_ref[TM:, :] = _adaln(halo, mod_ref, ng_ref, 1).astype(BF16)
    sub = lax.broadcasted_iota(jnp.int32, (SUBLANES, LRU_BW), 0)
    pad_before = [jnp.logical_or(is_ctx, first) if ch == 0 else is_ctx for ch in range(CHUNKS_PER_TILE)]
    pad_after = [jnp.logical_or(is_ctx, last) if ch == CHUNKS_PER_TILE - 1 else is_ctx
                 for ch in range(CHUNKS_PER_TILE)]

    def conv(n, ch, chunk, before, after):
        cols = slice(n * LRU_BW, (n + 1) * LRU_BW)
        cb = cw_ref[CONV_W:CONV_W + 1, cols]
        taps = [cw_ref[j:j + 1, cols] for j in range(CONV_W)]
        before = jnp.where(pad_before[ch], 0.0, before)
        after = jnp.where(pad_after[ch], 0.0, after)
        xp = _to_scan_order(chunk)
        xg = [xp[k * SUBLANES:(k + 1) * SUBLANES] for k in range(LRU_SEG)]
        xm1 = jnp.where(sub == 0, before, pltpu.roll(xg[LRU_SEG - 1], 1, 0))
        xp0 = jnp.where(sub == SUBLANES - 1, after[0:1], pltpu.roll(xg[0], SUBLANES - 1, 0))
        xp1 = jnp.where(sub == SUBLANES - 1, after[1:2], pltpu.roll(xg[1], SUBLANES - 1, 0))
        xe = [xm1] + xg + [xp0, xp1]
        for k in range(LRU_SEG):
            xc = cb + xe[k] * taps[0] + xe[k + 1] * taps[1] + xe[k + 2] * taps[2] + xe[k + 3] * taps[3]
            xc_ref[n, ch * LRU_T + k * SUBLANES:ch * LRU_T + (k + 1) * SUBLANES, :] = xc

    for c0 in range(0, D_RNN, PROJ_GROUP):
        wx = w_ref[:, D_RNN + c0:D_RNN + c0 + PROJ_GROUP]
        y = jnp.dot(hb_ref[...], wx, preferred_element_type=F32)
        g_ref[:, c0:c0 + PROJ_GROUP] = _gelu_tanh(jnp.dot(hb_ref[0:TM, :], w_ref[:, c0:c0 + PROJ_GROUP],
                                                          preferred_element_type=F32))
        for t in range(PROJ_GROUP // LRU_BW):
            lanes = slice(t * LRU_BW, (t + 1) * LRU_BW)
            ext = jnp.concatenate([y[TM:TM + HALO, lanes], y[0:TM, lanes], y[TM + HALO:, lanes]], axis=0)
            for ch in range(CHUNKS_PER_TILE):
                lo = HALO + ch * LRU_T
                conv(c0 // LRU_BW + t, ch, ext[lo:lo + LRU_T], ext[lo - 1:lo], ext[lo + LRU_T:lo + LRU_T + 2])


def _lru_in_proj(xs, mods_i, ng_i, w, cw):
    blocks_per_tile = TM // HALO
    n_halo_blocks = N_ALL // HALO
    return pl.pallas_call(
        _lru_in_kernel,
        grid=(N_TILES,),
        in_specs=[
            pl.BlockSpec((TM, D_MODEL), lambda i: (i, 0)),
            pl.BlockSpec((HALO, D_MODEL), lambda i: (jnp.maximum(i * blocks_per_tile - 1, 0), 0)),
            pl.BlockSpec((HALO, D_MODEL), lambda i: (jnp.minimum((i + 1) * blocks_per_tile, n_halo_blocks - 1), 0)),
            pl.BlockSpec((None, MOD_ROWS, D_MODEL), lambda i: (_tile_kind(i), 0, 0)),
            _const_spec((SUBLANES, D_MODEL)),
            _const_spec((D_MODEL, 2 * D_RNN)),
            _const_spec((SUBLANES, D_RNN)),
        ],
        out_specs=[pl.BlockSpec((TM, D_RNN), lambda i: (i, 0)),
                   pl.BlockSpec((LRU_BLOCKS, TM, LRU_BW), lambda i: (0, i, 0))],
        out_shape=[jax.ShapeDtypeStruct((N_ALL, D_RNN), F32),
                   jax.ShapeDtypeStruct((LRU_BLOCKS, N_ALL, LRU_BW), F32)],
        scratch_shapes=[pltpu.VMEM((TM + 2 * HALO, D_MODEL), BF16)],
        compiler_params=_cparams(("arbitrary",)),
        name="lru_in_proj",
    )(xs, xs, xs, mods_i, ng_i, w, cw)


def _softplus(x):
    return jnp.maximum(x, 0.0) + jnp.log1p(jnp.exp(-jnp.abs(x)))


def _lru_kernel(xf_ref, xb_ref, gwf_ref, gwb_ref, gbf_ref, gbb_ref, of_ref, ob_ref,
                h_ref, p_ref, carry_ref):
    @pl.when(pl.program_id(1) == 0)
    def _():
        carry_ref[...] = jnp.zeros_like(carry_ref)

    for n in range(LRU_BLOCKS):
        _lru_block(n, xf_ref, gwf_ref, gbf_ref, of_ref, h_ref.at[0], p_ref.at[0], carry_ref.at[0], False)
        _lru_block(n, xb_ref, gwb_ref, gbb_ref, ob_ref, h_ref.at[1], p_ref.at[1], carry_ref.at[1], True)


def _lru_block(n, xc_ref, gw_ref, gb_ref, o_ref, h_ref, p_ref, carry_ref, reverse):
    sub = lax.broadcasted_iota(jnp.int32, (SUBLANES, LRU_BW), 0)
    steps = range(LRU_SEG - 1, -1, -1) if reverse else range(LRU_SEG)
    segs = range(SUBLANES - 1, -1, -1) if reverse else range(SUBLANES)

    cols = slice(n * LRU_BW, (n + 1) * LRU_BW)
    xc = xc_ref[n]
    ri = jnp.dot(xc.astype(BF16), gw_ref[n], preferred_element_type=F32)
    r2 = jnp.tanh(ri[:, :LRU_BW] + gb_ref[0:1, cols]) + 1.0
    i2 = jnp.tanh(ri[:, LRU_BW:] + gb_ref[1:2, cols]) + 1.0
    log_a = r2 * ((-0.5 * LRU_C) * _softplus(-gb_ref[2:3, cols]))
    a = jnp.exp(log_a)
    t = jnp.tanh(log_a)
    z = (-0.5 * t) / (1.0 - t)
    b = jnp.where(z > 0.0, z * lax.rsqrt(z), 0.0) * (i2 * xc)

    h = jnp.zeros((SUBLANES, LRU_BW), F32)
    p = jnp.ones((SUBLANES, LRU_BW), F32)
    for k in steps:
        grp = slice(k * SUBLANES, (k + 1) * SUBLANES)
        h = a[grp] * h + b[grp]
        p = a[grp] * p
        h_ref[n, grp, :] = h
        p_ref[n, grp, :] = p
    st = carry_ref[n, 0:1, :]
    enter = jnp.zeros((SUBLANES, LRU_BW), F32)
    for s in segs:
        enter = jnp.where(sub == s, st, enter)
        st = p[s:s + 1, :] * st + h[s:s + 1, :]
    carry_ref[n, 0:1, :] = st
    for k in range(LRU_SEG):
        grp = slice(k * SUBLANES, (k + 1) * SUBLANES)
        o_ref[n, grp, :] = h_ref[n, grp, :] + p_ref[n, grp, :] * enter


def _lru_scan(xc, gw, gb):
    def row_block(reverse):
        def f(b, m):
            c = (X_CHUNKS - m) if reverse else (m - 1)
            return (0, jnp.where(m == 0, N_X // LRU_T + b, b * X_CHUNKS + c), 0)
        return f

    chunk = lambda reverse: pl.BlockSpec((LRU_BLOCKS, LRU_T, LRU_BW), row_block(reverse))
    gw_spec = _const_spec((LRU_BLOCKS, LRU_BW, 2 * LRU_BW))
    gb_spec = _const_spec((SUBLANES, D_RNN))
    out = jax.ShapeDtypeStruct((LRU_BLOCKS, N_ALL, LRU_BW), F32)
    return pl.pallas_call(
        _lru_kernel,
        grid=(BATCH, X_CHUNKS + 1),
        in_specs=[chunk(False), chunk(True), gw_spec, gw_spec, gb_spec, gb_spec],
        out_specs=[chunk(False), chunk(True)],
        out_shape=[out, out],
        scratch_shapes=[
            pltpu.VMEM((2, LRU_BLOCKS, LRU_T, LRU_BW), F32),
            pltpu.VMEM((2, LRU_BLOCKS, LRU_T, LRU_BW), F32),
            pltpu.VMEM((2, LRU_BLOCKS, SUBLANES, LRU_BW), F32),
        ],
        compiler_params=_cparams(("arbitrary", "arbitrary")),
        name="lru_scan",
    )(xc, xc, gw[0], gw[1], gb[0], gb[1])


SWA_TQ = 128
SWA_XQ = SEQ // SWA_TQ
SWA_QB = CTX_LEN // SWA_TQ
SWA_STEP = SWA_QB * SWA_TQ
SWA_XSTEPS = SEQ // SWA_STEP
SWA_SPAN = 3 * SWA_TQ
KV_COLS = 2 * B_KV_HEADS * LANES
_NT = (((1,), (1,)), ((), ()))


def _split_heads(qt):
    lo = lax.broadcasted_iota(jnp.int32, qt.shape, 1) < HEAD_DIM
    zero = jnp.zeros_like(qt)
    return jnp.concatenate([jnp.where(lo, qt, zero), jnp.where(lo, zero, qt)], axis=0)


def _merge_heads(o):
    t = o.shape[0] // 2
    lo = lax.broadcasted_iota(jnp.int32, (t, LANES), 1) < HEAD_DIM
    return jnp.where(lo, o[:t], o[t:])


def _swa_kernel(sink_ref, q_ref, kvp_ref, kvc_ref, kvn_ref, kvx_ref, o_ref):
    j = pl.program_id(1)
    is_x = j < SWA_XSTEPS
    iq = lax.broadcasted_iota(jnp.int32, (SWA_TQ, SWA_SPAN), 0)
    ik = lax.broadcasted_iota(jnp.int32, (SWA_TQ, SWA_SPAN), 1)
    band = jnp.where(jnp.abs(iq + WINDOW - ik) <= WINDOW, 0.0, NEG)
    biases = []
    for u in range(SWA_QB):
        jb = j * SWA_QB + u
        lo_ok = jnp.where(jb > 0, 0, SWA_TQ)
        hi_ok = jnp.where(jb < SWA_XQ - 1, SWA_SPAN, 2 * SWA_TQ)
        hi_ok = jnp.where(is_x, hi_ok, 0)
        b1 = jnp.where(ik >= lo_ok, band, NEG)
        b1 = jnp.where(ik < hi_ok, b1, NEG)
        biases.append(jnp.concatenate([b1, b1], axis=0))
    top = lax.broadcasted_iota(jnp.int32, (2 * SWA_TQ, 1), 0) < SWA_TQ
    for g in range(B_KV_HEADS):
        kcol = slice(g * LANES, (g + 1) * LANES)
        vcol = slice((B_KV_HEADS + g) * LANES, (B_KV_HEADS + g + 1) * LANES)
        own = [slice(u * SWA_TQ, (u + 1) * SWA_TQ) for u in range(SWA_QB)]
        kb = [kvp_ref[:, kcol]] + [kvc_ref[r, kcol] for r in own] + [kvn_ref[:, kcol]]
        vb = [kvp_ref[:, vcol]] + [kvc_ref[r, vcol] for r in own] + [kvn_ref[:, vcol]]
        kx = kvx_ref[:, kcol]
        vx = kvx_ref[:, vcol]
        for u in range(SWA_QB):
            kd = jnp.concatenate(kb[u:u + 3], axis=0)
            vd = jnp.concatenate(vb[u:u + 3], axis=0)
            for t in range(2):
                tile = 2 * g + t
                cols = slice(tile * LANES, (tile + 1) * LANES)
                qs = _split_heads(q_ref[own[u], cols])
                s_loc = lax.dot_general(qs, kd, _NT, preferred_element_type=F32) + biases[u]
                s_ctx = lax.dot_general(qs, kx, _NT, preferred_element_type=F32)
                sink = jnp.where(top, sink_ref[2 * tile], sink_ref[2 * tile + 1])
                mx = jnp.maximum(jnp.maximum(jnp.max(s_loc, axis=-1, keepdims=True),
                                             jnp.max(s_ctx, axis=-1, keepdims=True)), sink)
                p_loc = jnp.exp(s_loc - mx)
                p_ctx = jnp.exp(s_ctx - mx)
                den = (jnp.sum(p_loc, axis=-1, keepdims=True) + jnp.sum(p_ctx, axis=-1, keepdims=True)
                       + jnp.exp(sink - mx))
                o = (jnp.dot(p_loc.astype(BF16), vd, preferred_element_type=F32)
                     + jnp.dot(p_ctx.astype(BF16), vx, preferred_element_type=F32)) / den
                o_ref[own[u], cols] = _merge_heads(o).astype(BF16)


def _swa_attention(qkv, sinks):
    def q_block(b, j):
        return jnp.where(j < SWA_XSTEPS, b * SWA_XSTEPS + j, N_X // SWA_STEP + b)

    def edge_block(off):
        return lambda b, j: (b * SWA_XQ + jnp.clip(j * SWA_QB + off, 0, SWA_XQ - 1), 1)

    return pl.pallas_call(
        _swa_kernel,
        grid=(BATCH, SWA_XSTEPS + 1),
        in_specs=[
            pl.BlockSpec(memory_space=pltpu.SMEM),
            pl.BlockSpec((SWA_STEP, D_MODEL), lambda b, j: (q_block(b, j), 0)),
            pl.BlockSpec((SWA_TQ, KV_COLS), edge_block(-1)),
            pl.BlockSpec((SWA_STEP, KV_COLS), lambda b, j: (b * SWA_XSTEPS + jnp.minimum(j, SWA_XSTEPS - 1), 1)),
            pl.BlockSpec((SWA_TQ, KV_COLS), edge_block(SWA_QB)),
            pl.BlockSpec((CTX_LEN, KV_COLS), lambda b, j: (N_X // CTX_LEN + b, 1)),
        ],
        out_specs=pl.BlockSpec((SWA_STEP, D_MODEL), lambda b, j: (q_block(b, j), 0)),
        out_shape=jax.ShapeDtypeStruct((N_ALL, D_MODEL), BF16),
        compiler_params=_cparams(("arbitrary", "arbitrary")),
        name="swa_attn",
    )(sinks, qkv, qkv, qkv, qkv, qkv)


def _prep_swa_weights(w_qkv):
    nq = B_HEADS * HEAD_DIM
    nk = B_KV_HEADS * HEAD_DIM
    wq = w_qkv[:, :nq] * (HEAD_DIM ** -0.5)
    dup = lambda w: jnp.tile(w.reshape(D_MODEL, B_KV_HEADS, 1, HEAD_DIM), (1, 1, 2, 1)).reshape(D_MODEL, -1)
    wk = dup(w_qkv[:, nq:nq + nk])
    wv = dup(w_qkv[:, nq + nk:])
    return jnp.concatenate([wq, wk, wv], axis=1).astype(BF16)


NAT_ROWS = 4
NAT_TQ = NAT_ROWS * GRID_W
NAT_XQ = SEQ // NAT_TQ
NAT_SPAN = 3 * NAT_TQ
N_HEAD_TILES = C_HEADS * HEAD_DIM // LANES
GRID_ROWS = SEQ // GRID_W


NAT_KEY_ROWS = NAT_SPAN // GRID_W
NAT_RPB_ROWS = 2 * WIN_H
NAT_PATTERNS = 4


def _nat_pattern(j):
    return jnp.where(j == 0, 0, jnp.where(j == NAT_XQ - 1, 2, jnp.where(j == NAT_XQ, 3, 1)))


def _nat_kernel(sel_ref, q_ref, kp_ref, kc_ref, kn_ref, vp_ref, vc_ref, vn_ref, kx_ref, vx_ref, tab_ref,
                o_ref, bias_ref):
    j = pl.program_id(1)

    @pl.when(jnp.logical_or(j <= 1, j >= NAT_XQ - 1))
    def _():
        lo = lax.broadcasted_iota(jnp.int32, (GRID_W, LANES), 1) < GRID_W
        base = _nat_pattern(j) * (NAT_ROWS * NAT_KEY_ROWS)
        for qr in range(NAT_ROWS):
            rows = slice(qr * GRID_W, (qr + 1) * GRID_W)
            for kp in range(NAT_KEY_ROWS // 2):
                e0 = sel_ref[base + qr * NAT_KEY_ROWS + 2 * kp]
                e1 = sel_ref[base + qr * NAT_KEY_ROWS + 2 * kp + 1]

                def fill(h, carry, rows=rows, kp=kp, e0=e0, e1=e1):
                    bias_ref[h, rows, kp * LANES:(kp + 1) * LANES] = jnp.where(lo, tab_ref[h, e0], tab_ref[h, e1])
                    return carry

                lax.fori_loop(0, C_HEADS, fill, 0)

    for t in range(N_HEAD_TILES):
        cols = slice(t * LANES, (t + 1) * LANES)
        qs = _split_heads(q_ref[:, cols])
        kd = jnp.concatenate([kp_ref[:, cols], kc_ref[:, cols], kn_ref[:, cols]], axis=0)
        vd = jnp.concatenate([vp_ref[:, cols], vc_ref[:, cols], vn_ref[:, cols]], axis=0)
        bias = bias_ref[2 * t:2 * t + 2].reshape(2 * NAT_TQ, NAT_SPAN)
        s_loc = lax.dot_general(qs, kd, _NT, preferred_element_type=F32) + bias
        s_ctx = lax.dot_general(qs, kx_ref[:, cols], _NT, preferred_element_type=F32)
        mx = jnp.maximum(jnp.max(s_loc, axis=-1, keepdims=True), jnp.max(s_ctx, axis=-1, keepdims=True))
        p_loc = jnp.exp(s_loc - mx)
        p_ctx = jnp.exp(s_ctx - mx)
        den = jnp.sum(p_loc, axis=-1, keepdims=True) + jnp.sum(p_ctx, axis=-1, keepdims=True)
        o = (jnp.dot(p_loc.astype(BF16), vd, preferred_element_type=F32)
             + jnp.dot(p_ctx.astype(BF16), vx_ref[:, cols], preferred_element_type=F32)) / den
        o_ref[:, cols] = _merge_heads(o).astype(BF16)


def _nat_attention(qkv, row_sel, tab):
    nq = NAT_XQ + 1

    def q_block(b, j):
        return jnp.where(j < NAT_XQ, b * NAT_XQ + j, N_X // NAT_TQ + b)

    def kv_spec(off, part):
        return pl.BlockSpec(
            (NAT_TQ, D_MODEL), lambda b, j: (b * NAT_XQ + jnp.clip(j + off, 0, NAT_XQ - 1), part))

    def ctx_spec(part):
        return pl.BlockSpec((CTX_LEN, D_MODEL), lambda b, j: (N_X // CTX_LEN + b, part))

    return pl.pallas_call(
        _nat_kernel,
        grid=(BATCH, nq),
        in_specs=[
            pl.BlockSpec(memory_space=pltpu.SMEM),
            pl.BlockSpec((NAT_TQ, D_MODEL), lambda b, j: (q_block(b, j), 0)),
            kv_spec(-1, 1), kv_spec(0, 1), kv_spec(1, 1),
            kv_spec(-1, 2), kv_spec(0, 2), kv_spec(1, 2),
            ctx_spec(1), ctx_spec(2),
            _const_spec((C_HEADS, NAT_RPB_ROWS, GRID_W, LANES)),
        ],
        out_specs=pl.BlockSpec((NAT_TQ, D_MODEL), lambda b, j: (q_block(b, j), 0)),
        out_shape=jax.ShapeDtypeStruct((N_ALL, D_MODEL), BF16),
        scratch_shapes=[pltpu.VMEM((C_HEADS, NAT_TQ, NAT_SPAN), F32)],
        compiler_params=_cparams(("arbitrary", "arbitrary")),
        name="nat_attn",
    )(row_sel, qkv, qkv, qkv, qkv, qkv, qkv, qkv, qkv, qkv, tab)


def _nat_row_select():
    qr = np.arange(NAT_ROWS)
    kr = np.arange(NAT_KEY_ROWS) - NAT_ROWS
    row_sel = []
    for r0 in (0, NAT_ROWS, GRID_ROWS - NAT_ROWS):
        r = r0 + qr
        rs = np.clip(r - WIN_H // 2, 0, GRID_ROWS - WIN_H)
        k_abs = r0 + kr
        row_ok = (k_abs[None, :] >= rs[:, None]) & (k_abs[None, :] < rs[:, None] + WIN_H)
        row_sel.append(np.where(row_ok, k_abs[None, :] - r[:, None] + (WIN_H - 1), NAT_RPB_ROWS - 1))
    row_sel.append(np.full((NAT_ROWS, NAT_KEY_ROWS), NAT_RPB_ROWS - 1))
    return jnp.asarray(np.stack(row_sel).reshape(-1), jnp.int32)


def _nat_bias_tiles(rpb):
    qc = np.arange(GRID_W)[:, None]
    kc = np.arange(LANES)[None, :] % GRID_W
    col_start = np.clip(qc - WIN_W // 2, 0, GRID_W - WIN_W)
    col_ok = (kc >= col_start) & (kc < col_start + WIN_W)
    pick = (np.arange(2 * WIN_W - 1)[:, None, None] == (kc - qc + (WIN_W - 1))[None]).astype(np.float32)
    vals = jnp.einsum('hem,mqk->heqk', rpb, pick, precision=lax.Precision.HIGHEST)
    tiles = jnp.where(col_ok[None, None], vals, NEG)
    return jnp.concatenate([tiles, jnp.full((C_HEADS, 1, GRID_W, LANES), NEG, F32)], axis=1)


def _prep_nat_weights(w_qkv):
    nq = C_HEADS * HEAD_DIM
    return jnp.concatenate([w_qkv[:, :nq] * (HEAD_DIM ** -0.5), w_qkv[:, nq:]], axis=1).astype(BF16)


def _pad_rows(a, rows=SUBLANES):
    return jnp.pad(a, ((0, rows - a.shape[0]), (0, 0)))


def kernel(x, c, ctx, c_ctx, w_ada, b_ada, norm_g, w_ffn_gu, w_ffn_down, a_w_in, a_conv_w, a_conv_b,
           a_gate_w, a_gate_b, a_lambda, a_w_out, b_w_qkv, b_sinks, b_w_o, c_w_qkv, c_rpb, c_w_o, final_g):
    xs = x.reshape(N_X, D_MODEL)
    cvec = _pad_rows(jnp.concatenate([c, c_ctx[None, :]], axis=0))
    mods = _ada_mods(cvec, w_ada, b_ada)
    rope = _rope_tables()
    wgu, wd = w_ffn_gu.astype(BF16), w_ffn_down.astype(BF16)
    for i in range(DEPTH):
        kind, j = i % N_MIXERS, i // N_MIXERS
        last = i == DEPTH - 1
        ng = _pad_rows(norm_g[i])
        xs = _ffn(xs, mods[i], ng, wgu, wd, i, 0, 0, ctx=ctx.reshape(N_CTX, D_MODEL) if i == 0 else None)
        if kind == 0:
            cw = _pad_rows(jnp.concatenate([a_conv_w[j], a_conv_b[j][None, :]], axis=0))
            g_branch, xc = _lru_in_proj(xs, mods[i], ng, a_w_in[j].astype(BF16), cw)
            gb = [_pad_rows(jnp.concatenate([0.5 * a_gate_b[j, d], a_lambda[j, d][None, :]], axis=0))
                  for d in range(2)]
            gw = [(0.5 * jnp.concatenate([a_gate_w[j, d, 0], a_gate_w[j, d, 1]], axis=-1)).astype(BF16)
                  for d in range(2)]
            hf, hb = _lru_scan(xc, gw, gb)
            mixed = dict(lru=(hf, hb, g_branch), w_o=a_w_out[j].astype(BF16))
        elif kind == 1:
            qkv = _nmm(xs, mods[i], ng, _prep_swa_weights(b_w_qkv[j]), BF16, rope=rope,
                       rope_cols=B_HEADS * HEAD_DIM + B_KV_HEADS * LANES)
            mixed = dict(attn=_swa_attention(qkv, b_sinks[j]), w_o=b_w_o[j].astype(BF16))
        else:
            qkv = _nmm(xs, mods[i], ng, _prep_nat_weights(c_w_qkv[j]), BF16)
            o = _nat_attention(qkv, _nat_row_select(), _nat_bias_tiles(c_rpb[j]))
            mixed = dict(attn=o, w_o=c_w_o[j].astype(BF16))
        xs = _ffn(xs, mods[i], ng, wgu, wd, i, 1, 2, final_g=final_g if last else None, **mixed)
    return xs.reshape(BATCH, SEQ, D_MODEL)
```

```python
import functools

import numpy as np
import jax
import jax.numpy as jnp
from jax import lax
from jax.experimental import pallas as pl
from jax.experimental.pallas import tpu as pltpu

D_MODEL = 1024
BATCH = 2
SEQ = 16384
DEPTH = 4
GRID_W = 64
CTX_LEN = 256
N_MIXERS = 3
EPS = 1e-6
FFN_RES = 0.5
D_FF = 2816
D_RNN = 1536
LRU_BLOCKS = 12
LRU_BW = D_RNN // LRU_BLOCKS
CONV_W = 4
LRU_C = 8.0
HEAD_DIM = 64
B_HEADS = 16
B_KV_HEADS = 4
WINDOW = 128
ROPE_BASE = 10000.0
C_HEADS = 16
WIN_H = 8
WIN_W = 16
NEG = -1e30

LANES = 128
SUBLANES = 8
VMEM_LIMIT = 56 * 1024 * 1024

N_X = BATCH * SEQ
N_CTX = BATCH * CTX_LEN
N_ALL = N_X + N_CTX
TM = 512
N_TILES = N_ALL // TM
X_TILES = N_X // TM
TILES_PER_BATCH = SEQ // TM
MOD_ROWS = 16
FF_CHUNK = 256
N_FF_CHUNKS = D_FF // FF_CHUNK

F32 = jnp.float32
BF16 = jnp.bfloat16


def _cparams(sem):
    return pltpu.CompilerParams(dimension_semantics=sem, vmem_limit_bytes=VMEM_LIMIT)


def _const_spec(shape):
    nd = len(shape)
    return pl.BlockSpec(shape, lambda *_: (0,) * nd, pipeline_mode=pl.Buffered(1))


def _tile_kind(i):
    return jnp.minimum(i // TILES_PER_BATCH, BATCH)


def _adaln(x, mod_ref, ng_ref, sub):
    g = ng_ref[sub:sub + 1, :]
    shift = mod_ref[3 * sub:3 * sub + 1, :]
    scale = mod_ref[3 * sub + 1:3 * sub + 2, :]
    y = x * lax.rsqrt(jnp.mean(x * x, axis=-1, keepdims=True) + EPS)
    return (y * g) * (1.0 + scale) + shift


ADA_COLS = 1152


def _ada_kernel(c_ref, w_ref, b_ref, o_ref):
    c = c_ref[...]
    sc = c * jax.nn.sigmoid(c)
    o_ref[...] = jnp.dot(sc, w_ref[...], preferred_element_type=F32,
                         precision=lax.Precision.HIGHEST) + b_ref[...]


def _ada_mods(cvec, w_ada, b_ada):
    out = pl.pallas_call(
        _ada_kernel,
        grid=(DEPTH, 9 * D_MODEL // ADA_COLS),
        in_specs=[
            pl.BlockSpec((SUBLANES, D_MODEL), lambda i, j: (0, 0)),
            pl.BlockSpec((None, D_MODEL, ADA_COLS), lambda i, j: (i, 0, j)),
            pl.BlockSpec((None, 1, ADA_COLS), lambda i, j: (i, 0, j)),
        ],
        out_specs=pl.BlockSpec((None, SUBLANES, ADA_COLS), lambda i, j: (i, 0, j)),
        out_shape=jax.ShapeDtypeStruct((DEPTH, SUBLANES, 9 * D_MODEL), F32),
        compiler_params=_cparams(("arbitrary", "arbitrary")),
        name="ada_mods",
    )(cvec, w_ada, b_ada.reshape(DEPTH, 1, 9 * D_MODEL))
    mods = out.reshape(DEPTH, SUBLANES, 9, D_MODEL)[:, :BATCH + 1]
    return jnp.pad(mods, ((0, 0), (0, 0), (0, MOD_ROWS - 9), (0, 0)))


def _gelu_tanh(g):
    return 0.5 * g * (1.0 + jnp.tanh(np.sqrt(2.0 / np.pi).astype(np.float32) * (g + 0.044715 * (g * g * g))))


def _ffn_kernel(*refs, sub, mixer, final):
    x_ref, mod_ref, ng_ref, wgu_ref, wd_ref = refs[:5]
    rest = list(refs[5:])
    if mixer == "attn":
        a_ref, wo_ref = rest[:2]
        rest = rest[2:]
        mixed = a_ref[...]
    elif mixer == "lru":
        hf_ref, hbk_ref, g_ref, wo_ref = rest[:4]
        rest = rest[4:]
        h = jnp.concatenate(
            [jnp.concatenate([_from_scan_order(hf_ref[n, ch * LRU_T:(ch + 1) * LRU_T, :]
                                               + hbk_ref[n, ch * LRU_T:(ch + 1) * LRU_T, :])
                              for ch in range(CHUNKS_PER_TILE)], axis=0)
             for n in range(LRU_BLOCKS)], axis=1)
        mixed = (h * g_ref[...]).astype(BF16)
    elif mixer == "join":
        ctx_ref = rest.pop(0)
    fg_ref = rest.pop(0) if final else None
    o_ref, hb_ref, acc_ref, xm_ref = rest
    if mixer is None:
        xm_ref = x_ref
    elif mixer == "join":
        xm_ref[...] = jnp.where(pl.program_id(0) < X_TILES, x_ref[...], ctx_ref[...])
    else:
        y = jnp.dot(mixed, wo_ref[...], preferred_element_type=F32)
        xm_ref[...] = x_ref[...] + mod_ref[5:6, :] * y
    hb_ref[...] = _adaln(xm_ref[...], mod_ref, ng_ref, sub).astype(BF16)
    acc_ref[...] = jnp.zeros_like(acc_ref)

    for c in range(N_FF_CHUNKS):
        lo = c * FF_CHUNK
        g = jnp.dot(hb_ref[...], wgu_ref[:, lo:lo + FF_CHUNK], preferred_element_type=F32)
        u = jnp.dot(hb_ref[...], wgu_ref[:, D_FF + lo:D_FF + lo + FF_CHUNK], preferred_element_type=F32)
        a = ((g * jax.nn.sigmoid(g)) * u).astype(BF16)
        acc_ref[...] += jnp.dot(a, wd_ref[lo:lo + FF_CHUNK, :], preferred_element_type=F32)
    gate = mod_ref[3 * sub + 2:3 * sub + 3, :]
    y = xm_ref[...] + (FFN_RES * gate) * acc_ref[...]
    if final:
        y = (y * lax.rsqrt(jnp.mean(y * y, axis=-1, keepdims=True) + EPS)) * fg_ref[...]
    o_ref[...] = y


def _ffn(xs, mods_i, ng_i, wgu, wd, layer, which, sub, ctx=None, attn=None, lru=None, w_o=None, final_g=None):
    final = final_g is not None
    n_tiles = X_TILES if final else N_TILES
    w_map = lambda i: (layer, which, 0, 0)
    row_spec = lambda cols: pl.BlockSpec((TM, cols), lambda i: (i, 0))
    x_spec = row_spec(D_MODEL)
    if ctx is not None:
        x_spec = pl.BlockSpec((TM, D_MODEL), lambda i: (jnp.minimum(i, X_TILES - 1), 0))
    in_specs = [
        x_spec,
        pl.BlockSpec((None, MOD_ROWS, D_MODEL), lambda i: (_tile_kind(i), 0, 0)),
        _const_spec((SUBLANES, D_MODEL)),
        pl.BlockSpec((None, None, D_MODEL, 2 * D_FF), w_map, pipeline_mode=pl.Buffered(1)),
        pl.BlockSpec((None, None, D_FF, D_MODEL), w_map, pipeline_mode=pl.Buffered(1)),
    ]
    args = [xs, mods_i, ng_i, wgu, wd]
    mixer = None
    if attn is not None:
        mixer = "attn"
        in_specs += [row_spec(D_MODEL), _const_spec((D_MODEL, D_MODEL))]
        args += [attn, w_o]
    elif lru is not None:
        mixer = "lru"
        h_spec = pl.BlockSpec((LRU_BLOCKS, TM, LRU_BW), lambda i: (0, i, 0))
        in_specs += [h_spec, h_spec, row_spec(D_RNN), _const_spec((D_RNN, D_MODEL))]
        args += [*lru, w_o]
    elif ctx is not None:
        mixer = "join"
        in_specs.append(_const_spec((N_CTX, D_MODEL)))
        args.append(ctx)
    if final:
        in_specs.append(_const_spec((1, D_MODEL)))
        args.append(final_g.reshape(1, D_MODEL))
    scratch = [pltpu.VMEM((TM, D_MODEL), BF16), pltpu.VMEM((TM, D_MODEL), F32), pltpu.VMEM((TM, D_MODEL), F32)]
    return pl.pallas_call(
        functools.partial(_ffn_kernel, sub=sub, mixer=mixer, final=final),
        grid=(n_tiles,),
        in_specs=in_specs,
        out_specs=row_spec(D_MODEL),
        out_shape=jax.ShapeDtypeStruct((n_tiles * TM, D_MODEL), F32),
        scratch_shapes=scratch,
        compiler_params=_cparams(("arbitrary",)),
        name="ffn_" + (mixer or "plain") + ("_final" if final else ""),
    )(*args)


NMM_COLS = 512


def _rope_tile(y, c, s1, s2):
    return y * c + pltpu.roll(y, LANES - 16, 1) * s1 + pltpu.roll(y, 16, 1) * s2


def _nmm_kernel(*refs, n_cols, rope_cols, out_dtype):
    if rope_cols:
        x_ref, mod_ref, ng_ref, w_ref, rt_ref, ct_ref, o_ref = refs
        is_x = pl.program_id(0) < X_TILES

        def factor(k, ident):
            t = jnp.concatenate([ct_ref[k] + rt_ref[k, g:g + 1, :] for g in range(TM // GRID_W)], axis=0)
            return jnp.where(is_x, t, ident)

        c, s1, s2 = factor(0, 1.0), factor(1, 0.0), factor(2, 0.0)
    else:
        x_ref, mod_ref, ng_ref, w_ref, o_ref = refs
    hb = _adaln(x_ref[...], mod_ref, ng_ref, 1).astype(BF16)
    for c0 in range(0, n_cols, NMM_COLS):
        y = jnp.dot(hb, w_ref[:, c0:c0 + NMM_COLS], preferred_element_type=F32)
        if c0 < rope_cols:
            y = jnp.concatenate(
                [_rope_tile(y[:, t:t + LANES], c, s1, s2) for t in range(0, NMM_COLS, LANES)], axis=1)
        o_ref[:, c0:c0 + NMM_COLS] = y.astype(out_dtype)


def _nmm(xs, mods_i, ng_i, w, out_dtype, rope=None, rope_cols=0):
    n_cols = w.shape[1]
    in_specs = [
        pl.BlockSpec((TM, D_MODEL), lambda i: (i, 0)),
        pl.BlockSpec((None, MOD_ROWS, D_MODEL), lambda i: (_tile_kind(i), 0, 0)),
        _const_spec((SUBLANES, D_MODEL)),
        _const_spec((D_MODEL, n_cols)),
    ]
    args = [xs, mods_i, ng_i, w]
    if rope_cols:
        rows_per_tile = TM // GRID_W
        rmap = lambda i: (0, jnp.where(i < X_TILES, i % TILES_PER_BATCH, 0), 0)
        in_specs += [pl.BlockSpec((3, rows_per_tile, LANES), rmap), _const_spec((3, GRID_W, LANES))]
        args += list(rope)
    return pl.pallas_call(
        functools.partial(_nmm_kernel, n_cols=n_cols, rope_cols=rope_cols, out_dtype=out_dtype),
        grid=(N_TILES,),
        in_specs=in_specs,
        out_specs=pl.BlockSpec((TM, n_cols), lambda i: (i, 0)),
        out_shape=jax.ShapeDtypeStruct((N_ALL, n_cols), out_dtype),
        compiler_params=_cparams(("arbitrary",)),
        name="adaln_proj",
    )(*args)


def _rope_tables():
    quarter = HEAD_DIM // 4
    inv = ROPE_BASE ** (-jnp.arange(quarter, dtype=F32) / quarter)
    heads = LANES // HEAD_DIM

    def tables(n, row_half):
        ang = jnp.arange(n).astype(F32)[:, None] * inv
        c, s = jnp.cos(ang), jnp.sin(ang)
        z = jnp.zeros_like(s)
        lay = lambda a, b: jnp.concatenate(([a, b, z, z] if row_half else [z, z, a, b]) * heads, axis=1)
        return jnp.stack([lay(c, c), lay(-s, z), lay(z, s)])

    return tables(GRID_ROWS, True), tables(GRID_W, False)


LRU_T = 256
LRU_SEG = LRU_T // SUBLANES
X_CHUNKS = SEQ // LRU_T
HALO = SUBLANES
CHUNKS_PER_TILE = TM // LRU_T
PROJ_GROUP = 256


def _to_scan_order(v):
    return jnp.swapaxes(v.reshape(SUBLANES, LRU_SEG, LRU_BW), 0, 1).reshape(LRU_T, LRU_BW)


def _from_scan_order(v):
    return jnp.swapaxes(v.reshape(LRU_SEG, SUBLANES, LRU_BW), 0, 1).reshape(LRU_T, LRU_BW)


def _lru_in_kernel(x_ref, xp_ref, xn_ref, mod_ref, ng_ref, w_ref, cw_ref, g_ref, xc_ref, hb_ref):
    i = pl.program_id(0)
    is_ctx = i >= X_TILES
    first = i % TILES_PER_BATCH == 0
    last = i % TILES_PER_BATCH == TILES_PER_BATCH - 1
    hb_ref[0:TM, :] = _adaln(x_ref[...], mod_ref, ng_ref, 1).astype(BF16)
    halo = jnp.concatenate([xp_ref[...], xn_ref[...]], axis=0)
    hb_ref[TM:, :] = _adaln(halo, mod_ref, ng_ref, 1).astype(BF16)
    sub = lax.broadcasted_iota(jnp.int32, (SUBLANES, LRU_BW), 0)
    pad_before = [jnp.logical_or(is_ctx, first) if ch == 0 else is_ctx for ch in range(CHUNKS_PER_TILE)]
    pad_after = [jnp.logical_or(is_ctx, last) if ch == CHUNKS_PER_TILE - 1 else is_ctx
                 for ch in range(CHUNKS_PER_TILE)]

    def conv(n, ch, chunk, before, after):
        cols = slice(n * LRU_BW, (n + 1) * LRU_BW)
        cb = cw_ref[CONV_W:CONV_W + 1, cols]
        taps = [cw_ref[j:j + 1, cols] for j in range(CONV_W)]
        before = jnp.where(pad_before[ch], 0.0, before)
        after = jnp.where(pad_after[ch], 0.0, after)
        xp = _to_scan_order(chunk)
        xg = [xp[k * SUBLANES:(k + 1) * SUBLANES] for k in range(LRU_SEG)]
        xm1 = jnp.where(sub == 0, before, pltpu.roll(xg[LRU_SEG - 1], 1, 0))
        xp0 = jnp.where(sub == SUBLANES - 1, after[0:1], pltpu.roll(xg[0], SUBLANES - 1, 0))
        xp1 = jnp.where(sub == SUBLANES - 1, after[1:2], pltpu.roll(xg[1], SUBLANES - 1, 0))
        xe = [xm1] + xg + [xp0, xp1]
        for k in range(LRU_SEG):
            xc = cb + xe[k] * taps[0] + xe[k + 1] * taps[1] + xe[k + 2] * taps[2] + xe[k + 3] * taps[3]
            xc_ref[n, ch * LRU_T + k * SUBLANES:ch * LRU_T + (k + 1) * SUBLANES, :] = xc

    for c0 in range(0, D_RNN, PROJ_GROUP):
        wx = w_ref[:, D_RNN + c0:D_RNN + c0 + PROJ_GROUP]
        y = jnp.dot(hb_ref[...], wx, preferred_element_type=F32)
        g_ref[:, c0:c0 + PROJ_GROUP] = _gelu_tanh(jnp.dot(hb_ref[0:TM, :], w_ref[:, c0:c0 + PROJ_GROUP],
                                                          preferred_element_type=F32))
        for t in range(PROJ_GROUP // LRU_BW):
            lanes = slice(t * LRU_BW, (t + 1) * LRU_BW)
            ext = jnp.concatenate([y[TM:TM + HALO, lanes], y[0:TM, lanes], y[TM + HALO:, lanes]], axis=0)
            for ch in range(CHUNKS_PER_TILE):
                lo = HALO + ch * LRU_T
                conv(c0 // LRU_BW + t, ch, ext[lo:lo + LRU_T], ext[lo - 1:lo], ext[lo + LRU_T:lo + LRU_T + 2])


def _lru_in_proj(xs, mods_i, ng_i, w, cw):
    blocks_per_tile = TM // HALO
    n_halo_blocks = N_ALL // HALO
    return pl.pallas_call(
        _lru_in_kernel,
        grid=(N_TILES,),
        in_specs=[
            pl.BlockSpec((TM, D_MODEL), lambda i: (i, 0)),
            pl.BlockSpec((HALO, D_MODEL), lambda i: (jnp.maximum(i * blocks_per_tile - 1, 0), 0)),
            pl.BlockSpec((HALO, D_MODEL), lambda i: (jnp.minimum((i + 1) * blocks_per_tile, n_halo_blocks - 1), 0)),
            pl.BlockSpec((None, MOD_ROWS, D_MODEL), lambda i: (_tile_kind(i), 0, 0)),
            _const_spec((SUBLANES, D_MODEL)),
            _const_spec((D_MODEL, 2 * D_RNN)),
            _const_spec((SUBLANES, D_RNN)),
        ],
        out_specs=[pl.BlockSpec((TM, D_RNN), lambda i: (i, 0)),
                   pl.BlockSpec((LRU_BLOCKS, TM, LRU_BW), lambda i: (0, i, 0))],
        out_shape=[jax.ShapeDtypeStruct((N_ALL, D_RNN), F32),
                   jax.ShapeDtypeStruct((LRU_BLOCKS, N_ALL, LRU_BW), F32)],
        scratch_shapes=[pltpu.VMEM((TM + 2 * HALO, D_MODEL), BF16)],
        compiler_params=_cparams(("arbitrary",)),
        name="lru_in_proj",
    )(xs, xs, xs, mods_i, ng_i, w, cw)


def _softplus(x):
    return jnp.maximum(x, 0.0) + jnp.log1p(jnp.exp(-jnp.abs(x)))


def _lru_kernel(xf_ref, xb_ref, gwf_ref, gwb_ref, gbf_ref, gbb_ref, of_ref, ob_ref,
                h_ref, p_ref, carry_ref):
    @pl.when(pl.program_id(1) == 0)
    def _():
        carry_ref[...] = jnp.zeros_like(carry_ref)

    for n in range(LRU_BLOCKS):
        _lru_block(n, xf_ref, gwf_ref, gbf_ref, of_ref, h_ref.at[0], p_ref.at[0], carry_ref.at[0], False)
        _lru_block(n, xb_ref, gwb_ref, gbb_ref, ob_ref, h_ref.at[1], p_ref.at[1], carry_ref.at[1], True)


def _lru_block(n, xc_ref, gw_ref, gb_ref, o_ref, h_ref, p_ref, carry_ref, reverse):
    sub = lax.broadcasted_iota(jnp.int32, (SUBLANES, LRU_BW), 0)
    steps = range(LRU_SEG - 1, -1, -1) if reverse else range(LRU_SEG)
    segs = range(SUBLANES - 1, -1, -1) if reverse else range(SUBLANES)

    cols = slice(n * LRU_BW, (n + 1) * LRU_BW)
    xc = xc_ref[n]
    ri = jnp.dot(xc.astype(BF16), gw_ref[n], preferred_element_type=F32)
    r2 = jnp.tanh(ri[:, :LRU_BW] + gb_ref[0:1, cols]) + 1.0
    i2 = jnp.tanh(ri[:, LRU_BW:] + gb_ref[1:2, cols]) + 1.0
    log_a = r2 * ((-0.5 * LRU_C) * _softplus(-gb_ref[2:3, cols]))
    a = jnp.exp(log_a)
    t = jnp.tanh(log_a)
    z = (-0.5 * t) / (1.0 - t)
    b = jnp.where(z > 0.0, z * lax.rsqrt(z), 0.0) * (i2 * xc)

    h = jnp.zeros((SUBLANES, LRU_BW), F32)
    p = jnp.ones((SUBLANES, LRU_BW), F32)
    for k in steps:
        grp = slice(k * SUBLANES, (k + 1) * SUBLANES)
        h = a[grp] * h + b[grp]
        p = a[grp] * p
        h_ref[n, grp, :] = h
        p_ref[n, grp, :] = p
    st = carry_ref[n, 0:1, :]
    enter = jnp.zeros((SUBLANES, LRU_BW), F32)
    for s in segs:
        enter = jnp.where(sub == s, st, enter)
        st = p[s:s + 1, :] * st + h[s:s + 1, :]
    carry_ref[n, 0:1, :] = st
    for k in range(LRU_SEG):
        grp = slice(k * SUBLANES, (k + 1) * SUBLANES)
        o_ref[n, grp, :] = h_ref[n, grp, :] + p_ref[n, grp, :] * enter


def _lru_scan(xc, gw, gb):
    def row_block(reverse):
        def f(b, m):
            c = (X_CHUNKS - m) if reverse else (m - 1)
            return (0, jnp.where(m == 0, N_X // LRU_T + b, b * X_CHUNKS + c), 0)
        return f

    chunk = lambda reverse: pl.BlockSpec((LRU_BLOCKS, LRU_T, LRU_BW), row_block(reverse))
    gw_spec = _const_spec((LRU_BLOCKS, LRU_BW, 2 * LRU_BW))
    gb_spec = _const_spec((SUBLANES, D_RNN))
    out = jax.ShapeDtypeStruct((LRU_BLOCKS, N_ALL, LRU_BW), F32)
    return pl.pallas_call(
        _lru_kernel,
        grid=(BATCH, X_CHUNKS + 1),
        in_specs=[chunk(False), chunk(True), gw_spec, gw_spec, gb_spec, gb_spec],
        out_specs=[chunk(False), chunk(True)],
        out_shape=[out, out],
        scratch_shapes=[
            pltpu.VMEM((2, LRU_BLOCKS, LRU_T, LRU_BW), F32),
            pltpu.VMEM((2, LRU_BLOCKS, LRU_T, LRU_BW), F32),
            pltpu.VMEM((2, LRU_BLOCKS, SUBLANES, LRU_BW), F32),
        ],
        compiler_params=_cparams(("arbitrary", "arbitrary")),
        name="lru_scan",
    )(xc, xc, gw[0], gw[1], gb[0], gb[1])


SWA_TQ = 128
SWA_XQ = SEQ // SWA_TQ
SWA_QB = CTX_LEN // SWA_TQ
SWA_STEP = SWA_QB * SWA_TQ
SWA_XSTEPS = SEQ // SWA_STEP
SWA_SPAN = 3 * SWA_TQ
KV_COLS = 2 * B_KV_HEADS * LANES
_NT = (((1,), (1,)), ((), ()))


def _split_heads(qt):
    lo = lax.broadcasted_iota(jnp.int32, qt.shape, 1) < HEAD_DIM
    zero = jnp.zeros_like(qt)
    return jnp.concatenate([jnp.where(lo, qt, zero), jnp.where(lo, zero, qt)], axis=0)


def _merge_heads(o):
    t = o.shape[0] // 2
    lo = lax.broadcasted_iota(jnp.int32, (t, LANES), 1) < HEAD_DIM
    return jnp.where(lo, o[:t], o[t:])


def _swa_kernel(sink_ref, q_ref, kvp_ref, kvc_ref, kvn_ref, kvx_ref, o_ref):
    j = pl.program_id(1)
    is_x = j < SWA_XSTEPS
    iq = lax.broadcasted_iota(jnp.int32, (SWA_TQ, SWA_SPAN), 0)
    ik = lax.broadcasted_iota(jnp.int32, (SWA_TQ, SWA_SPAN), 1)
    band = jnp.where(jnp.abs(iq + WINDOW - ik) <= WINDOW, 0.0, NEG)
    biases = []
    for u in range(SWA_QB):
        jb = j * SWA_QB + u
        lo_ok = jnp.where(jb > 0, 0, SWA_TQ)
        hi_ok = jnp.where(jb < SWA_XQ - 1, SWA_SPAN, 2 * SWA_TQ)
        hi_ok = jnp.where(is_x, hi_ok, 0)
        b1 = jnp.where(ik >= lo_ok, band, NEG)
        b1 = jnp.where(ik < hi_ok, b1, NEG)
        biases.append(jnp.concatenate([b1, b1], axis=0))
    top = lax.broadcasted_iota(jnp.int32, (2 * SWA_TQ, 1), 0) < SWA_TQ
    for g in range(B_KV_HEADS):
        kcol = slice(g * LANES, (g + 1) * LANES)
        vcol = slice((B_KV_HEADS + g) * LANES, (B_KV_HEADS + g + 1) * LANES)
        own = [slice(u * SWA_TQ, (u + 1) * SWA_TQ) for u in range(SWA_QB)]
        kb = [kvp_ref[:, kcol]] + [kvc_ref[r, kcol] for r in own] + [kvn_ref[:, kcol]]
        vb = [kvp_ref[:, vcol]] + [kvc_ref[r, vcol] for r in own] + [kvn_ref[:, vcol]]
        kx = kvx_ref[:, kcol]
        vx = kvx_ref[:, vcol]
        probs = [(u, 2 * g + t) for u in range(SWA_QB) for t in range(2)]
        kds = [jnp.concatenate(kb[u:u + 3], axis=0) for u in range(SWA_QB)]
        vds = [jnp.concatenate(vb[u:u + 3], axis=0) for u in range(SWA_QB)]
        scores = []
        for u, tile in probs:
            qs = _split_heads(q_ref[own[u], tile * LANES:(tile + 1) * LANES])
            scores.append((lax.dot_general(qs, kds[u], _NT, preferred_element_type=F32) + biases[u],
                           lax.dot_general(qs, kx, _NT, preferred_element_type=F32)))
        weights = []
        for (u, tile), (s_loc, s_ctx) in zip(probs, scores):
            sink = jnp.where(top, sink_ref[2 * tile], sink_ref[2 * tile + 1])
            mx = jnp.maximum(jnp.maximum(jnp.max(s_loc, axis=-1, keepdims=True),
                                         jnp.max(s_ctx, axis=-1, keepdims=True)), sink)
            p_loc = jnp.exp(s_loc - mx)
            p_ctx = jnp.exp(s_ctx - mx)
            den = (jnp.sum(p_loc, axis=-1, keepdims=True) + jnp.sum(p_ctx, axis=-1, keepdims=True)
                   + jnp.exp(sink - mx))
            weights.append((p_loc.astype(BF16), p_ctx.astype(BF16), den))
        for (u, tile), (p_loc, p_ctx, den) in zip(probs, weights):
            o = (jnp.dot(p_loc, vds[u], preferred_element_type=F32)
                 + jnp.dot(p_ctx, vx, preferred_element_type=F32)) / den
            o_ref[own[u], tile * LANES:(tile + 1) * LANES] = _merge_heads(o).astype(BF16)


def _swa_attention(qkv, sinks):
    def q_block(b, j):
        return jnp.where(j < SWA_XSTEPS, b * SWA_XSTEPS + j, N_X // SWA_STEP + b)

    def edge_block(off):
        return lambda b, j: (b * SWA_XQ + jnp.clip(j * SWA_QB + off, 0, SWA_XQ - 1), 1)

    return pl.pallas_call(
        _swa_kernel,
        grid=(BATCH, SWA_XSTEPS + 1),
        in_specs=[
            pl.BlockSpec(memory_space=pltpu.SMEM),
            pl.BlockSpec((SWA_STEP, D_MODEL), lambda b, j: (q_block(b, j), 0)),
            pl.BlockSpec((SWA_TQ, KV_COLS), edge_block(-1)),
            pl.BlockSpec((SWA_STEP, KV_COLS), lambda b, j: (b * SWA_XSTEPS + jnp.minimum(j, SWA_XSTEPS - 1), 1)),
            pl.BlockSpec((SWA_TQ, KV_COLS), edge_block(SWA_QB)),
            pl.BlockSpec((CTX_LEN, KV_COLS), lambda b, j: (N_X // CTX_LEN + b, 1)),
        ],
        out_specs=pl.BlockSpec((SWA_STEP, D_MODEL), lambda b, j: (q_block(b, j), 0)),
        out_shape=jax.ShapeDtypeStruct((N_ALL, D_MODEL), BF16),
        compiler_params=_cparams(("arbitrary", "arbitrary")),
        name="swa_attn",
    )(sinks, qkv, qkv, qkv, qkv, qkv)


def _prep_swa_weights(w_qkv):
    nq = B_HEADS * HEAD_DIM
    nk = B_KV_HEADS * HEAD_DIM
    wq = w_qkv[:, :nq] * (HEAD_DIM ** -0.5)
    dup = lambda w: jnp.tile(w.reshape(D_MODEL, B_KV_HEADS, 1, HEAD_DIM), (1, 1, 2, 1)).reshape(D_MODEL, -1)
    wk = dup(w_qkv[:, nq:nq + nk])
    wv = dup(w_qkv[:, nq + nk:])
    return jnp.concatenate([wq, wk, wv], axis=1).astype(BF16)


NAT_ROWS = 4
NAT_TQ = NAT_ROWS * GRID_W
NAT_XQ = SEQ // NAT_TQ
NAT_SPAN = 3 * NAT_TQ
N_HEAD_TILES = C_HEADS * HEAD_DIM // LANES
GRID_ROWS = SEQ // GRID_W


NAT_KEY_ROWS = NAT_SPAN // GRID_W
NAT_RPB_ROWS = 2 * WIN_H
NAT_PATTERNS = 4


def _nat_pattern(j):
    return jnp.where(j == 0, 0, jnp.where(j == NAT_XQ - 1, 2, jnp.where(j == NAT_XQ, 3, 1)))


def _nat_kernel(sel_ref, q_ref, kp_ref, kc_ref, kn_ref, vp_ref, vc_ref, vn_ref, kx_ref, vx_ref, tab_ref,
                o_ref, bias_ref):
    j = pl.program_id(1)

    @pl.when(jnp.logical_or(j <= 1, j >= NAT_XQ - 1))
    def _():
        lo = lax.broadcasted_iota(jnp.int32, (GRID_W, LANES), 1) < GRID_W
        base = _nat_pattern(j) * (NAT_ROWS * NAT_KEY_ROWS)
        for qr in range(NAT_ROWS):
            rows = slice(qr * GRID_W, (qr + 1) * GRID_W)
            for kp in range(NAT_KEY_ROWS // 2):
                e0 = sel_ref[base + qr * NAT_KEY_ROWS + 2 * kp]
                e1 = sel_ref[base + qr * NAT_KEY_ROWS + 2 * kp + 1]

                def fill(h, carry, rows=rows, kp=kp, e0=e0, e1=e1):
                    bias_ref[h, rows, kp * LANES:(kp + 1) * LANES] = jnp.where(lo, tab_ref[h, e0], tab_ref[h, e1])
                    return carry

                lax.fori_loop(0, C_HEADS, fill, 0)

    for t in range(N_HEAD_TILES):
        cols = slice(t * LANES, (t + 1) * LANES)
        qs = _split_heads(q_ref[:, cols])
        kd = jnp.concatenate([kp_ref[:, cols], kc_ref[:, cols], kn_ref[:, cols]], axis=0)
        vd = jnp.concatenate([vp_ref[:, cols], vc_ref[:, cols], vn_ref[:, cols]], axis=0)
        bias = bias_ref[2 * t:2 * t + 2].reshape(2 * NAT_TQ, NAT_SPAN)
        s_loc = lax.dot_general(qs, kd, _NT, preferred_element_type=F32) + bias
        s_ctx = lax.dot_general(qs, kx_ref[:, cols], _NT, preferred_element_type=F32)
        mx = jnp.maximum(jnp.max(s_loc, axis=-1, keepdims=True), jnp.max(s_ctx, axis=-1, keepdims=True))
        p_loc = jnp.exp(s_loc - mx)
        p_ctx = jnp.exp(s_ctx - mx)
        den = jnp.sum(p_loc, axis=-1, keepdims=True) + jnp.sum(p_ctx, axis=-1, keepdims=True)
        o = (jnp.dot(p_loc.astype(BF16), vd, preferred_element_type=F32)
             + jnp.dot(p_ctx.astype(BF16), vx_ref[:, cols], preferred_element_type=F32)) / den
        o_ref[:, cols] = _merge_heads(o).astype(BF16)


def _nat_attention(qkv, row_sel, tab):
    nq = NAT_XQ + 1

    def q_block(b, j):
        return jnp.where(j < NAT_XQ, b * NAT_XQ + j, N_X // NAT_TQ + b)

    def kv_spec(off, part):
        return pl.BlockSpec(
            (NAT_TQ, D_MODEL), lambda b, j: (b * NAT_XQ + jnp.clip(j + off, 0, NAT_XQ - 1), part))

    def ctx_spec(part):
        return pl.BlockSpec((CTX_LEN, D_MODEL), lambda b, j: (N_X // CTX_LEN + b, part))

    return pl.pallas_call(
        _nat_kernel,
        grid=(BATCH, nq),
        in_specs=[
            pl.BlockSpec(memory_space=pltpu.SMEM),
            pl.BlockSpec((NAT_TQ, D_MODEL), lambda b, j: (q_block(b, j), 0)),
            kv_spec(-1, 1), kv_spec(0, 1), kv_spec(1, 1),
            kv_spec(-1, 2), kv_spec(0, 2), kv_spec(1, 2),
            ctx_spec(1), ctx_spec(2),
            _const_spec((C_HEADS, NAT_RPB_ROWS, GRID_W, LANES)),
        ],
        out_specs=pl.BlockSpec((NAT_TQ, D_MODEL), lambda b, j: (q_block(b, j), 0)),
        out_shape=jax.ShapeDtypeStruct((N_ALL, D_MODEL), BF16),
        scratch_shapes=[pltpu.VMEM((C_HEADS, NAT_TQ, NAT_SPAN), F32)],
        compiler_params=_cparams(("arbitrary", "arbitrary")),
        name="nat_attn",
    )(row_sel, qkv, qkv, qkv, qkv, qkv, qkv, qkv, qkv, qkv, tab)


def _nat_row_select():
    qr = np.arange(NAT_ROWS)
    kr = np.arange(NAT_KEY_ROWS) - NAT_ROWS
    row_sel = []
    for r0 in (0, NAT_ROWS, GRID_ROWS - NAT_ROWS):
        r = r0 + qr
        rs = np.clip(r - WIN_H // 2, 0, GRID_ROWS - WIN_H)
        k_abs = r0 + kr
        row_ok = (k_abs[None, :] >= rs[:, None]) & (k_abs[None, :] < rs[:, None] + WIN_H)
        row_sel.append(np.where(row_ok, k_abs[None, :] - r[:, None] + (WIN_H - 1), NAT_RPB_ROWS - 1))
    row_sel.append(np.full((NAT_ROWS, NAT_KEY_ROWS), NAT_RPB_ROWS - 1))
    return jnp.asarray(np.stack(row_sel).reshape(-1), jnp.int32)


def _nat_bias_tiles(rpb):
    qc = np.arange(GRID_W)[:, None]
    kc = np.arange(LANES)[None, :] % GRID_W
    col_start = np.clip(qc - WIN_W // 2, 0, GRID_W - WIN_W)
    col_ok = (kc >= col_start) & (kc < col_start + WIN_W)
    pick = (np.arange(2 * WIN_W - 1)[:, None, None] == (kc - qc + (WIN_W - 1))[None]).astype(np.float32)
    vals = jnp.einsum('hem,mqk->heqk', rpb, pick, precision=lax.Precision.HIGHEST)
    tiles = jnp.where(col_ok[None, None], vals, NEG)
    return jnp.concatenate([tiles, jnp.full((C_HEADS, 1, GRID_W, LANES), NEG, F32)], axis=1)


def _prep_nat_weights(w_qkv):
    nq = C_HEADS * HEAD_DIM
    return jnp.concatenate([w_qkv[:, :nq] * (HEAD_DIM ** -0.5), w_qkv[:, nq:]], axis=1).astype(BF16)


def _pad_rows(a, rows=SUBLANES):
    return jnp.pad(a, ((0, rows - a.shape[0]), (0, 0)))


def kernel(x, c, ctx, c_ctx, w_ada, b_ada, norm_g, w_ffn_gu, w_ffn_down, a_w_in, a_conv_w, a_conv_b,
           a_gate_w, a_gate_b, a_lambda, a_w_out, b_w_qkv, b_sinks, b_w_o, c_w_qkv, c_rpb, c_w_o, final_g):
    xs = x.reshape(N_X, D_MODEL)
    cvec = _pad_rows(jnp.concatenate([c, c_ctx[None, :]], axis=0))
    mods = _ada_mods(cvec, w_ada, b_ada)
    rope = _rope_tables()
    wgu, wd = w_ffn_gu.astype(BF16), w_ffn_down.astype(BF16)
    for i in range(DEPTH):
        kind, j = i % N_MIXERS, i // N_MIXERS
        last = i == DEPTH - 1
        ng = _pad_rows(norm_g[i])
        xs = _ffn(xs, mods[i], ng, wgu, wd, i, 0, 0, ctx=ctx.reshape(N_CTX, D_MODEL) if i == 0 else None)
        if kind == 0:
            cw = _pad_rows(jnp.concatenate([a_conv_w[j], a_conv_b[j][None, :]], axis=0))
            g_branch, xc = _lru_in_proj(xs, mods[i], ng, a_w_in[j].astype(BF16), cw)
            gb = [_pad_rows(jnp.concatenate([0.5 * a_gate_b[j, d], a_lambda[j, d][None, :]], axis=0))
                  for d in range(2)]
            gw = [(0.5 * jnp.concatenate([a_gate_w[j, d, 0], a_gate_w[j, d, 1]], axis=-1)).astype(BF16)
                  for d in range(2)]
            hf, hb = _lru_scan(xc, gw, gb)
            mixed = dict(lru=(hf, hb, g_branch), w_o=a_w_out[j].astype(BF16))
        elif kind == 1:
            qkv = _nmm(xs, mods[i], ng, _prep_swa_weights(b_w_qkv[j]), BF16, rope=rope,
                       rope_cols=B_HEADS * HEAD_DIM + B_KV_HEADS * LANES)
            mixed = dict(attn=_swa_attention(qkv, b_sinks[j]), w_o=b_w_o[j].astype(BF16))
        else:
            qkv = _nmm(xs, mods[i], ng, _prep_nat_weights(c_w_qkv[j]), BF16)
            o = _nat_attention(qkv, _nat_row_select(), _nat_bias_tiles(c_rpb[j]))
            mixed = dict(attn=o, w_o=c_w_o[j].astype(BF16))
        xs = _ffn(xs, mods[i], ng, wgu, wd, i, 1, 2, final_g=final_g if last else None, **mixed)
    return xs.reshape(BATCH, SEQ, D_MODEL)
```
